```python
import jax, jax.numpy as jnp
from jax import lax
import numpy as np

D_MODEL = 1024
BATCH = 2
SEQ = 8192
DEPTH = 4
DEC_BATCH = 128
DEC_SEQ = 1
PAST_LEN = 8192
PAGE_SIZE = 128

HEAD_DIM = 64
N_A_HEADS = 8
A_WIDTH = N_A_HEADS * HEAD_DIM
N_Q_HEADS = 8
N_KV_HEADS = 2
GQA_GROUP = N_Q_HEADS // N_KV_HEADS
B_WIDTH = N_Q_HEADS * HEAD_DIM
KV_WIDTH = N_KV_HEADS * HEAD_DIM
MIX_WIDTH = A_WIDTH + B_WIDTH
DECAY_LORA = 64
AAA_LORA = 64
GATE_LORA = 160
A_PROJ = 3 * A_WIDTH + DECAY_LORA + AAA_LORA + GATE_LORA
B_PROJ = B_WIDTH + 2 * KV_WIDTH
IN_PROJ = A_PROJ + B_PROJ
WINDOW = 128
BLOCK = WINDOW
WIN_BUF = min(WINDOW, PAST_LEN)
D_FF = -(-8 * D_MODEL // (3 * 256)) * 256
ROPE_THETA = 10000.0
NORM_EPS = 1e-5
LNX_EPS = 64e-5
ATTN_SCALE = HEAD_DIM ** -0.5

kernel_name = "hybrid_rwkv7_swa_sink_decoder_step"

F32 = jnp.float32


def rmsnorm(x, g):
    xf = x.astype(F32)
    y = xf * lax.rsqrt(jnp.mean(xf * xf, axis=-1, keepdims=True) + NORM_EPS)
    return (y * g.astype(F32)).astype(x.dtype)


def rope(x, positions):
    half = HEAD_DIM // 2
    inv = ROPE_THETA ** (-jnp.arange(half, dtype=F32) / half)
    ang = positions.astype(F32)[:, None] * inv[None, :]
    cos = jnp.cos(ang)[:, None, :]
    sin = jnp.sin(ang)[:, None, :]
    xf = x.astype(F32)
    x1, x2 = xf[..., :half], xf[..., half:]
    return jnp.concatenate([x1 * cos - x2 * sin, x2 * cos + x1 * sin], axis=-1).astype(x.dtype)


def sink_softmax(s, mask, sink):
    s = jnp.where(mask, s, -jnp.inf)
    m = jnp.maximum(jnp.max(s, axis=-1, keepdims=True), sink)
    p = jnp.exp(s - m)
    return p / (jnp.sum(p, axis=-1, keepdims=True) + jnp.exp(sink - m))


def rwkv7_scan(r, decay, k, v, a_, b_, S0):
    def step(S, inp):
        r_t, w_t, k_t, v_t, a_t, b_t = inp
        sa = jnp.einsum('bhij,bhj->bhi', S, a_t)
        S = S * w_t[:, :, None, :] + sa[..., None] * b_t[:, :, None, :] + v_t[..., None] * k_t[:, :, None, :]
        y = jnp.einsum('bhij,bhj->bhi', S, r_t)
        return S, y
    seq = tuple(jnp.swapaxes(t, 0, 1) for t in (r, decay, k, v, a_, b_))
    S, y = lax.scan(step, S0, seq)
    return jnp.swapaxes(y, 0, 1), S


def rwkv7_mix(pa, shift_prev, S0, W, l):
    Bn, T, _ = pa.shape
    prev = jnp.concatenate([shift_prev[:, None].astype(pa.dtype), pa[:, :-1]], axis=1)
    xs = pa + W['mu'][l] * (prev - pa)
    r, k, v, wlo, alo, glo = jnp.split(
        xs, [A_WIDTH, 2 * A_WIDTH, 3 * A_WIDTH, 3 * A_WIDTH + DECAY_LORA, 3 * A_WIDTH + DECAY_LORA + AAA_LORA], axis=-1)
    w = -jax.nn.softplus(-(W['w0'][l] + jnp.tanh(wlo) @ W['w_decay_up'][l]).astype(F32)) - 0.5
    decay = jnp.exp(-jnp.exp(w))
    a = jax.nn.sigmoid((W['a0'][l] + alo @ W['w_a_up'][l]).astype(F32))
    g = jax.nn.sigmoid(glo) @ W['w_g_up'][l]
    hs = lambda t: t.reshape(Bn, T, N_A_HEADS, HEAD_DIM)
    kk = hs((k * W['k_k'][l]).astype(F32))
    kk = kk / jnp.maximum(jnp.sqrt(jnp.sum(kk * kk, axis=-1, keepdims=True)), 1e-12)
    k_mod = k.astype(F32) * (1.0 + (a - 1.0) * W['k_a'][l].astype(F32))
    rh, kh, vh, ah, dh = hs(r.astype(F32)), hs(k_mod), hs(v.astype(F32)), hs(a), hs(decay)
    y, S = rwkv7_scan(rh, dh, kh, vh, -kk, kk * ah, S0.astype(F32))
    mean = jnp.mean(y, axis=-1, keepdims=True)
    var = jnp.mean(jnp.square(y - mean), axis=-1, keepdims=True)
    yn = ((y - mean) * lax.rsqrt(var + LNX_EPS)).reshape(Bn, T, A_WIDTH)
    yn = yn * W['lnx_g'][l].astype(F32) + W['lnx_b'][l].astype(F32)
    bonus = (jnp.sum(rh * kh * W['r_k'][l].astype(F32), axis=-1, keepdims=True) * vh).reshape(Bn, T, A_WIDTH)
    out = (yn + bonus) * g.astype(F32)
    return out.astype(pa.dtype), pa[:, -1], S


def swa_banded(q, k, v, sink):
    Bn, T = q.shape[0], q.shape[1]
    nb = T // BLOCK
    qb = q.reshape(Bn, nb, BLOCK, N_KV_HEADS, GQA_GROUP, HEAD_DIM)

    def band(t):
        tb = t.reshape(Bn, nb, BLOCK, N_KV_HEADS, HEAD_DIM)
        prev = jnp.concatenate([jnp.zeros_like(tb[:, :1]), tb[:, :-1]], axis=1)
        return jnp.concatenate([prev, tb], axis=2)

    kb, vb = band(k), band(v)
    s = jnp.einsum('bnqkgd,bnskd->bnkgqs', qb, kb).astype(F32) * ATTN_SCALE
    i = jnp.arange(BLOCK)
    j = jnp.arange(2 * BLOCK)
    n = jnp.arange(nb)
    diff = i[:, None] - j[None, :] + BLOCK
    kpos = n[:, None] * BLOCK - BLOCK + j[None, :]
    mask = ((diff >= 0) & (diff < WINDOW))[None] & (kpos >= 0)[:, None, :]
    mask = mask[None, :, None, None]
    p = sink_softmax(s, mask, sink[None, None, :, :, None, None])
    o = jnp.einsum('bnkgqs,bnskd->bnqkgd', p.astype(vb.dtype), vb)
    return o.reshape(Bn, T, B_WIDTH)


def swa_step(q, k, v, kbuf, vbuf, sink, positions):
    Bn, T = q.shape[0], q.shape[1]
    kc = jnp.concatenate([kbuf.astype(k.dtype), k], axis=1)
    vc = jnp.concatenate([vbuf.astype(v.dtype), v], axis=1)
    qg = q.reshape(Bn, T, N_KV_HEADS, GQA_GROUP, HEAD_DIM)
    s = jnp.einsum('btkgd,bskd->bkgts', qg, kc).astype(F32) * ATTN_SCALE
    kpos = positions[0] - WIN_BUF + jnp.arange(WIN_BUF + T)
    diff = positions[:, None] - kpos[None, :]
    mask = ((diff >= 0) & (diff < WINDOW))[None, None, None]
    p = sink_softmax(s, mask, sink[None, :, :, None, None])
    o = jnp.einsum('bkgts,bskd->btkgd', p.astype(vc.dtype), vc).reshape(Bn, T, B_WIDTH)
    return o, kc[:, -WIN_BUF:], vc[:, -WIN_BUF:]


def decoder_layer(x, l, positions, shift_prev, S0, kbuf, vbuf, W):
    Bn, T, _ = x.shape
    h = rmsnorm(x, W['attn_norm'][l])
    proj = h @ W['w_in'][l]
    pa, pb = proj[..., :A_PROJ], proj[..., A_PROJ:]
    ya, new_shift, S_new = rwkv7_mix(pa, shift_prev, S0, W, l)
    q = pb[..., :B_WIDTH].reshape(Bn, T, N_Q_HEADS, HEAD_DIM)
    k = pb[..., B_WIDTH:B_WIDTH + KV_WIDTH].reshape(Bn, T, N_KV_HEADS, HEAD_DIM)
    v = pb[..., B_WIDTH + KV_WIDTH:].reshape(Bn, T, N_KV_HEADS, HEAD_DIM)
    q = rope(q, positions)
    k = rope(k, positions)
    sink = W['sinks'][l].reshape(N_KV_HEADS, GQA_GROUP).astype(F32)
    if kbuf is None:
        yb = swa_banded(q, k, v, sink)
        new_k, new_v = k[:, T - WIN_BUF:], v[:, T - WIN_BUF:]
    else:
        yb, new_k, new_v = swa_step(q, k, v, kbuf, vbuf, sink, positions)
    mixed = jnp.concatenate([ya, yb.astype(ya.dtype)], axis=-1) @ W['w_out'][l]
    x = x + mixed
    h2 = rmsnorm(x, W['ffn_norm'][l])
    x = x + (jax.nn.silu(h2 @ W['w_gate'][l]) * (h2 @ W['w_up'][l])) @ W['w_down'][l]
    return x, new_shift, S_new, new_k, new_v


def setup_inputs(seed: int = 0) -> dict:
    key = jax.random.key(seed)
    ks = iter(jax.random.split(key, 32))
    nrm = lambda shape, scale: scale * jax.random.normal(next(ks), shape, F32)
    x_prompt = nrm((BATCH, SEQ, D_MODEL), 1.0)
    x_sample = nrm((DEC_BATCH, DEC_SEQ, D_MODEL), 1.0)
    state_rwkv = nrm((DEPTH, DEC_BATCH, N_A_HEADS, HEAD_DIM, HEAD_DIM), 0.2)
    state_shift = nrm((DEPTH, DEC_BATCH, A_PROJ), 1.0)
    cache_k_win = nrm((DEPTH, DEC_BATCH, WIN_BUF, N_KV_HEADS, HEAD_DIM), 1.0)
    cache_v_win = nrm((DEPTH, DEC_BATCH, WIN_BUF, N_KV_HEADS, HEAD_DIM), 1.0)
    attn_norm = 1.0 + nrm((DEPTH, D_MODEL), 0.02)
    w_in = nrm((DEPTH, D_MODEL, IN_PROJ), D_MODEL ** -0.5)
    mu = jax.random.uniform(next(ks), (DEPTH, A_PROJ), F32)
    w0 = jax.random.uniform(next(ks), (DEPTH, A_WIDTH), F32, -6.0, -0.5)
    w_decay_up = nrm((DEPTH, DECAY_LORA, A_WIDTH), 0.1 * DECAY_LORA ** -0.5)
    a0 = nrm((DEPTH, A_WIDTH), 0.1)
    w_a_up = nrm((DEPTH, AAA_LORA, A_WIDTH), 0.5 * AAA_LORA ** -0.5)
    w_g_up = nrm((DEPTH, GATE_LORA, A_WIDTH), GATE_LORA ** -0.5)
    k_k = 0.85 + nrm((DEPTH, A_WIDTH), 0.05)
    k_a = 1.0 + nrm((DEPTH, A_WIDTH), 0.05)
    r_k = nrm((DEPTH, N_A_HEADS, HEAD_DIM), 0.1)
    lnx_g = 1.0 + nrm((DEPTH, A_WIDTH), 0.02)
    lnx_b = nrm((DEPTH, A_WIDTH), 0.02)
    sinks = nrm((DEPTH, N_Q_HEADS), 0.5)
    w_out = nrm((DEPTH, MIX_WIDTH, D_MODEL), MIX_WIDTH ** -0.5)
    ffn_norm = 1.0 + nrm((DEPTH, D_MODEL), 0.02)
    w_gate = nrm((DEPTH, D_MODEL, D_FF), D_MODEL ** -0.5)
    w_up = nrm((DEPTH, D_MODEL, D_FF), D_MODEL ** -0.5)
    w_down = nrm((DEPTH, D_FF, D_MODEL), D_FF ** -0.5)
    final_norm = 1.0 + nrm((D_MODEL,), 0.02)
    return {"x_prompt": x_prompt, "x_sample": x_sample, "state_rwkv": state_rwkv, "state_shift": state_shift,
            "cache_k_win": cache_k_win, "cache_v_win": cache_v_win, "attn_norm": attn_norm, "w_in": w_in,
            "mu": mu, "w0": w0, "w_decay_up": w_decay_up, "a0": a0, "w_a_up": w_a_up, "w_g_up": w_g_up,
            "k_k": k_k, "k_a": k_a, "r_k": r_k, "lnx_g": lnx_g, "lnx_b": lnx_b, "sinks": sinks,
            "w_out": w_out, "ffn_norm": ffn_norm, "w_gate": w_gate, "w_up": w_up, "w_down": w_down,
            "final_norm": final_norm}


def reference(x_prompt, x_sample, state_rwkv, state_shift, cache_k_win, cache_v_win, attn_norm, w_in, mu, w0,
              w_decay_up, a0, w_a_up, w_g_up, k_k, k_a, r_k, lnx_g, lnx_b, sinks, w_out, ffn_norm, w_gate, w_up,
              w_down, final_norm):
    W = dict(attn_norm=attn_norm, w_in=w_in, mu=mu, w0=w0, w_decay_up=w_decay_up, a0=a0, w_a_up=w_a_up,
             w_g_up=w_g_up, k_k=k_k, k_a=k_a, r_k=r_k, lnx_g=lnx_g, lnx_b=lnx_b, sinks=sinks, w_out=w_out,
             ffn_norm=ffn_norm, w_gate=w_gate, w_up=w_up, w_down=w_down)
    Bp, Tp = x_prompt.shape[0], x_prompt.shape[1]
    Ts = x_sample.shape[1]
    pos_p = jnp.arange(Tp, dtype=jnp.int32)
    pos_s = PAST_LEN + jnp.arange(Ts, dtype=jnp.int32)
    xp, xs = x_prompt, x_sample
    p_S, p_sh, p_k, p_v = [], [], [], []
    s_S, s_sh, s_k, s_v = [], [], [], []
    for l in range(DEPTH):
        shift0 = jnp.zeros((Bp, A_PROJ), xp.dtype)
        S0 = jnp.zeros((Bp, N_A_HEADS, HEAD_DIM, HEAD_DIM), F32)
        xp, sh, S, kw, vw = decoder_layer(xp, l, pos_p, shift0, S0, None, None, W)
        p_S.append(S); p_sh.append(sh); p_k.append(kw); p_v.append(vw)
        xs, sh, S, kw, vw = decoder_layer(xs, l, pos_s, state_shift[l], state_rwkv[l], cache_k_win[l],
                                          cache_v_win[l], W)
        s_S.append(S); s_sh.append(sh); s_k.append(kw); s_v.append(vw)
    y_prompt = rmsnorm(xp, final_norm)
    y_sample = rmsnorm(xs, final_norm)
    return (y_prompt, y_sample, jnp.stack(p_S), jnp.stack(p_sh), jnp.stack(p_k), jnp.stack(p_v),
            jnp.stack(s_S), jnp.stack(s_sh), jnp.stack(s_k), jnp.stack(s_v))
```

```python
import functools

import jax
import jax.numpy as jnp
from jax import lax
from jax.experimental import pallas as pl
from jax.experimental.pallas import tpu as pltpu

F32 = jnp.float32
BF16 = jnp.bfloat16

D_MODEL = 1024
HEAD_DIM = 64
N_A_HEADS = 8
A_WIDTH = N_A_HEADS * HEAD_DIM
N_Q_HEADS = 8
N_KV_HEADS = 2
GQA_GROUP = N_Q_HEADS // N_KV_HEADS
B_WIDTH = N_Q_HEADS * HEAD_DIM
KV_WIDTH = N_KV_HEADS * HEAD_DIM
DECAY_LORA = 64
AAA_LORA = 64
GATE_LORA = 160
A_PROJ = 3 * A_WIDTH + DECAY_LORA + AAA_LORA + GATE_LORA
WINDOW = 128
PAST_LEN = 8192
D_FF = 2816
ROPE_THETA = 10000.0
NORM_EPS = 1e-5
LNX_EPS = 64e-5
ATTN_SCALE = HEAD_DIM ** -0.5

LANES = 128
HEAD_PAIRS = N_A_HEADS // 2
CHUNK = 64
RKV_W = 3 * A_WIDTH
LORA_W = 512
LORA_A_OFF = 128
LORA_G_OFF = 256
PROJ_W = RKV_W + LORA_W + B_WIDTH + 2 * KV_WIDTH
VMEM_LIMIT = 48 * 1024 * 1024

_HI = lax.Precision.HIGHEST


def _dot(a, b, precision=None):
    return jnp.dot(a, b, precision=precision, preferred_element_type=F32)


def _dot_nt(a, b, precision=None):
    return lax.dot_general(a, b, (((1,), (1,)), ((), ())), precision=precision,
                           preferred_element_type=F32)


def _iota(shape, dim):
    return lax.broadcasted_iota(jnp.int32, shape, dim)


def _rms(x, g):
    return x * lax.rsqrt(jnp.mean(x * x, axis=-1, keepdims=True) + NORM_EPS) * g


def _head_sum_matrix():
    return (_iota((LANES, LANES), 0) // HEAD_DIM == _iota((LANES, LANES), 1) // HEAD_DIM).astype(F32)


def _params(sem):
    return pltpu.CompilerParams(dimension_semantics=sem, vmem_limit_bytes=VMEM_LIMIT)


def _inproj_kernel(x_ref, g_ref, w_ref, cos_ref, sin_ref, rkv_ref, lo_ref, q_ref, k_ref, v_ref):
    hb = _rms(x_ref[...], g_ref[...]).astype(BF16)
    rkv_ref[...] = _dot(hb, w_ref[:, 0:RKV_W])
    lo_ref[...] = _dot(hb, w_ref[:, RKV_W:RKV_W + LORA_W])
    qkv = _dot(hb, w_ref[:, RKV_W + LORA_W:PROJ_W])
    cos = cos_ref[...]
    sin = sin_ref[...]
    first_half = (_iota((1, LANES), 1) % HEAD_DIM) < (HEAD_DIM // 2)

    def rope(t):
        rot = jnp.where(first_half, pltpu.roll(t, LANES - HEAD_DIM // 2, 1), pltpu.roll(t, HEAD_DIM // 2, 1))
        return t * cos + rot * sin

    for j in range(B_WIDTH // LANES):
        q_ref[:, j * LANES:(j + 1) * LANES] = rope(qkv[:, j * LANES:(j + 1) * LANES])
    k_ref[...] = rope(qkv[:, B_WIDTH:B_WIDTH + KV_WIDTH])
    v_ref[...] = qkv[:, B_WIDTH + KV_WIDTH:]


def _in_proj(x, gamma, w_all, cos, sin, tm):
    n = x.shape[0]
    nrope = cos.shape[0] // tm
    row = lambda w: pl.BlockSpec((tm, w), lambda i: (i, 0))
    rope_row = pl.BlockSpec((tm, LANES), lambda i: (i % nrope, 0))
    const = lambda a: pl.BlockSpec(a.shape, lambda i: (0,) * a.ndim)
    sds = lambda w: jax.ShapeDtypeStruct((n, w), F32)
    return pl.pallas_call(
        _inproj_kernel, grid=(n // tm,),
        in_specs=[row(D_MODEL), const(gamma), const(w_all), rope_row, rope_row],
        out_specs=[row(RKV_W), row(LORA_W), row(B_WIDTH), row(KV_WIDTH), row(KV_WIDTH)],
        out_shape=[sds(RKV_W), sds(LORA_W), sds(B_WIDTH), sds(KV_WIDTH), sds(KV_WIDTH)],
        compiler_params=_params(("parallel",)), name="in_proj",
    )(x, gamma, w_all, cos, sin)


def _rwkv_prep(xs_rkv, xs_lo, w0, wd, a0, wa, wg, k_k, k_a):
    r = xs_rkv[:, 0:A_WIDTH]
    k = xs_rkv[:, A_WIDTH:2 * A_WIDTH]
    v = xs_rkv[:, 2 * A_WIDTH:3 * A_WIDTH]
    z = w0 + _dot(jnp.tanh(xs_lo[:, 0:LORA_A_OFF]), wd)
    nz = -z
    softplus = jnp.maximum(nz, 0.0) + jnp.log1p(jnp.exp(-jnp.abs(nz)))
    logw = -jnp.exp(-softplus - 0.5)
    a = jax.nn.sigmoid(a0 + _dot(xs_lo[:, LORA_A_OFF:LORA_G_OFF], wa))
    g = _dot(jax.nn.sigmoid(xs_lo[:, LORA_G_OFF:LORA_W]), wg)
    kk = k * k_k
    hsum = _head_sum_matrix()
    parts = []
    for j in range(A_WIDTH // LANES):
        t = kk[:, j * LANES:(j + 1) * LANES]
        nrm = jnp.sqrt(_dot(t * t, hsum, _HI))
        parts.append(t / jnp.maximum(nrm, 1e-12))
    kk = jnp.concatenate(parts, axis=1)
    k_mod = k * (1.0 + (a - 1.0) * k_a)
    return r, logw, k_mod, v, -kk, kk * a, g


def _rwkv_post(y, r, k_mod, v, g, r_k, lnx_g, lnx_b):
    hsum = _head_sum_matrix()
    parts = []
    for j in range(A_WIDTH // LANES):
        ln = slice(j * LANES, (j + 1) * LANES)
        yj = y[:, ln]
        mean = _dot(yj, hsum, _HI) * (1.0 / HEAD_DIM)
        d = yj - mean
        var = _dot(d * d, hsum, _HI) * (1.0 / HEAD_DIM)
        yn = d * lax.rsqrt(var + LNX_EPS) * lnx_g[:, ln] + lnx_b[:, ln]
        bonus = _dot(r[:, ln] * k_mod[:, ln] * r_k[:, ln], hsum, _HI) * v[:, ln]
        parts.append((yn + bonus) * g[:, ln])
    return jnp.concatenate(parts, axis=1)


def _rwkv_chunk_kernel(rkv_ref, lo_ref, mu_rkv_ref, mu_lo_ref, w0_ref, wd_ref, a0_ref, wa_ref, wg_ref,
                       kk_ref, ka_ref, rk_ref, lng_ref, lnb_ref,
                       ya_ref, s_out_ref,
                       s_scr, carry_rkv, carry_lo, r_s, lw_s, km_s, v_s, an_s, bn_s, g_s, y_s, *, tc, nt):
    t = pl.program_id(1)

    @pl.when(t == 0)
    def _():
        s_scr[...] = jnp.zeros_like(s_scr)
        carry_rkv[...] = jnp.zeros_like(carry_rkv)
        carry_lo[...] = jnp.zeros_like(carry_lo)

    rkv = rkv_ref[...]
    lo = lo_ref[...]
    row0 = _iota((tc, 1), 0) == 0
    prev_rkv = jnp.where(row0, carry_rkv[7:8, :], pltpu.roll(rkv, 1, 0))
    prev_lo = jnp.where(row0, carry_lo[7:8, :], pltpu.roll(lo, 1, 0))
    carry_rkv[...] = rkv[tc - 8:tc, :]
    carry_lo[...] = lo[tc - 8:tc, :]
    xs_rkv = rkv + mu_rkv_ref[...] * (prev_rkv - rkv)
    xs_lo = lo + mu_lo_ref[...] * (prev_lo - lo)
    r, logw, k_mod, v, a_neg, b_pos, g = _rwkv_prep(
        xs_rkv, xs_lo, w0_ref[...], wd_ref[...], a0_ref[...], wa_ref[...], wg_ref[...], kk_ref[...], ka_ref[...])
    r_s[...] = r
    lw_s[...] = logw
    km_s[...] = k_mod
    v_s[...] = v
    an_s[...] = a_neg
    bn_s[...] = b_pos
    g_s[...] = g

    ci = _iota((CHUNK, CHUNK), 0)
    cj = _iota((CHUNK, CHUNK), 1)
    tri_incl = (ci >= cj).astype(F32)
    ri = _iota((LANES, LANES), 0)
    rj = _iota((LANES, LANES), 1)
    same = (ri // CHUNK) == (rj // CHUNK)
    strict = same & (ri > rj)
    incl = same & (ri >= rj)
    eye = (ri == rj).astype(F32)
    lo_lanes = _iota((1, LANES), 1) < HEAD_DIM

    def stack(x):
        return jnp.concatenate([jnp.where(lo_lanes, x, 0.0), jnp.where(lo_lanes, 0.0, x)], axis=0)

    def chunk(c, carry):
        rows = pl.ds(pl.multiple_of(c * CHUNK, CHUNK), CHUNK)
        for p in range(HEAD_PAIRS):
            ln = slice(p * LANES, (p + 1) * LANES)
            lw = lw_s[rows, ln]
            cum = _dot(tri_incl, lw, _HI)
            w_incl = jnp.exp(cum)
            w_excl = jnp.exp(cum - lw)
            w_inv = jnp.exp(-cum)
            a2 = stack(an_s[rows, ln] * w_excl)
            r2 = stack(r_s[rows, ln] * w_incl)
            b2 = stack(bn_s[rows, ln] * w_inv)
            k2 = stack(km_s[rows, ln] * w_inv)
            v2 = stack(v_s[rows, ln])
            s2 = s_scr[p]
            gram = _dot_nt(jnp.concatenate([a2, r2], axis=0), jnp.concatenate([b2, k2], axis=0), _HI)
            a_ab = jnp.where(strict, gram[0:LANES, 0:LANES], 0.0)
            a_ak = jnp.where(strict, gram[0:LANES, LANES:], 0.0)
            q_rb = jnp.where(incl, gram[LANES:, 0:LANES], 0.0)
            q_rk = jnp.where(incl, gram[LANES:, LANES:], 0.0)
            npow = a_ab
            tinv = eye + a_ab
            for _ in range(5):
                npow = _dot(npow, npow, _HI)
                tinv = tinv + _dot(tinv, npow, _HI)
            u2 = _dot(tinv, _dot_nt(a2, s2, _HI) + _dot(a_ak, v2, _HI), _HI)
            y2 = _dot_nt(r2, s2, _HI) + _dot(q_rb, u2, _HI) + _dot(q_rk, v2, _HI)
            y_s[rows, ln] = y2[0:CHUNK, :] + y2[CHUNK:, :]
            s_new = s2 + _dot(u2.T, b2, _HI) + _dot(v2.T, k2, _HI)
            s_scr[p] = s_new * w_incl[CHUNK - 1:CHUNK, :]
        return carry

    lax.fori_loop(0, tc // CHUNK, chunk, 0)

    ya_ref[...] = _rwkv_post(y_s[...], r_s[...], km_s[...], v_s[...], g_s[...],
                             rk_ref[...], lng_ref[...], lnb_ref[...])

    @pl.when(t == nt - 1)
    def _():
        for p in range(HEAD_PAIRS):
            s2 = s_scr[p]
            s_out_ref[0, 2 * p] = s2[0:HEAD_DIM, 0:HEAD_DIM]
            s_out_ref[0, 2 * p + 1] = s2[HEAD_DIM:, HEAD_DIM:]


def _rwkv_chunk(rkv, lo, wl, nb, tc):
    n = rkv.shape[0]
    nt = n // nb // tc
    row = lambda w: pl.BlockSpec((tc, w), lambda b, t: (b * nt + t, 0))
    const = lambda a: pl.BlockSpec(a.shape, lambda b, t: (0,) * a.ndim)
    consts = [wl[k] for k in ("mu_rkv", "mu_lo", "w0", "wd", "a0", "wa", "wg", "k_k", "k_a", "r_k", "lnx_g", "lnx_b")]
    wide = lambda: pltpu.VMEM((tc, A_WIDTH), F32)
    return pl.pallas_call(
        functools.partial(_rwkv_chunk_kernel, tc=tc, nt=nt), grid=(nb, nt),
        in_specs=[row(RKV_W), row(LORA_W)] + [const(a) for a in consts],
        out_specs=[row(A_WIDTH), pl.BlockSpec((1, N_A_HEADS, HEAD_DIM, HEAD_DIM), lambda b, t: (b, 0, 0, 0))],
        out_shape=[jax.ShapeDtypeStruct((n, A_WIDTH), F32),
                   jax.ShapeDtypeStruct((nb, N_A_HEADS, HEAD_DIM, HEAD_DIM), F32)],
        scratch_shapes=[pltpu.VMEM((HEAD_PAIRS, LANES, LANES), F32), pltpu.VMEM((8, RKV_W), F32),
                        pltpu.VMEM((8, LORA_W), F32)] + [wide() for _ in range(8)],
        compiler_params=_params(("arbitrary", "arbitrary")), name="rwkv_chunk",
    )(rkv, lo, *consts)


def _rwkv_step_kernel(rkv_ref, lo_ref, prkv_ref, plo_ref, s_ref, mu_rkv_ref, mu_lo_ref, w0_ref, wd_ref, a0_ref,
                      wa_ref, wg_ref, kk_ref, ka_ref, rk_ref, lng_ref, lnb_ref,
                      ya_ref, s_out_ref,
                      r_s, w_s, km_s, v_s, an_s, bn_s, y_s, *, bb):
    rkv = rkv_ref[...]
    lo = lo_ref[...]
    xs_rkv = rkv + mu_rkv_ref[...] * (prkv_ref[...] - rkv)
    xs_lo = lo + mu_lo_ref[...] * (plo_ref[...] - lo)
    r, logw, k_mod, v, a_neg, b_pos, g = _rwkv_prep(
        xs_rkv, xs_lo, w0_ref[...], wd_ref[...], a0_ref[...], wa_ref[...], wg_ref[...], kk_ref[...], ka_ref[...])
    r_s[...] = r
    w_s[...] = jnp.exp(logw)
    km_s[...] = k_mod
    v_s[...] = v
    an_s[...] = a_neg
    bn_s[...] = b_pos
    eye = _iota((HEAD_DIM, HEAD_DIM), 0) == _iota((HEAD_DIM, HEAD_DIM), 1)

    for bi in range(bb):
        row = slice(bi, bi + 1)
        for h in range(N_A_HEADS):
            ln = slice(h * HEAD_DIM, (h + 1) * HEAD_DIM)
            s = s_ref[bi, h]
            sa = jnp.sum(s * an_s[row, ln], axis=-1, keepdims=True)
            v_col = jnp.sum(jnp.where(eye, v_s[row, ln], 0.0), axis=-1, keepdims=True)
            s = s * w_s[row, ln] + sa * bn_s[row, ln] + v_col * km_s[row, ln]
            s_out_ref[bi, h] = s
            y_col = jnp.sum(s * r_s[row, ln], axis=-1, keepdims=True)
            y_s[row, ln] = jnp.sum(jnp.where(eye, y_col, 0.0), axis=0, keepdims=True)
    ya_ref[...] = _rwkv_post(y_s[...], r, k_mod, v, g, rk_ref[...], lng_ref[...], lnb_ref[...])


def _rwkv_step(rkv, lo, prev_rkv, prev_lo, state, wl, bb):
    n = rkv.shape[0]
    row = lambda w: pl.BlockSpec((bb, w), lambda i: (i, 0))
    st = pl.BlockSpec((bb, N_A_HEADS, HEAD_DIM, HEAD_DIM), lambda i: (i, 0, 0, 0))
    const = lambda a: pl.BlockSpec(a.shape, lambda i: (0,) * a.ndim)
    consts = [wl[k] for k in ("mu_rkv", "mu_lo", "w0", "wd", "a0", "wa", "wg", "k_k", "k_a", "r_k", "lnx_g", "lnx_b")]
    return pl.pallas_call(
        functools.partial(_rwkv_step_kernel, bb=bb), grid=(n // bb,),
        in_specs=[row(RKV_W), row(LORA_W), row(RKV_W), row(LORA_W), st] + [const(a) for a in consts],
        out_specs=[row(A_WIDTH), st],
        out_shape=[jax.ShapeDtypeStruct((n, A_WIDTH), F32), jax.ShapeDtypeStruct(state.shape, F32)],
        scratch_shapes=[pltpu.VMEM((bb, A_WIDTH), F32) for _ in range(7)],
        compiler_params=_params(("parallel",)), name="rwkv_step",
    )(rkv, lo, prev_rkv, prev_lo, state, *consts)


def _dup_kv(x):
    lo_lanes = _iota((1, LANES), 1) < HEAD_DIM
    xr = pltpu.roll(x, HEAD_DIM, 1)
    return [jnp.where(lo_lanes, x, xr), jnp.where(lo_lanes, xr, x)]


def _sink_attention(q, kdup, vdup, sink_ref, mask):
    lo_lanes = _iota((1, LANES), 1) < HEAD_DIM
    outs = []
    for j in range(B_WIDTH // LANES):
        g = (2 * j) // GQA_GROUP
        q128 = q[:, j * LANES:(j + 1) * LANES]
        halves = []
        for e in range(2):
            sink = sink_ref[2 * j + e]
            qm = jnp.where(lo_lanes, q128, 0.0) if e == 0 else jnp.where(lo_lanes, 0.0, q128)
            s = _dot_nt(qm, kdup[g]) * ATTN_SCALE
            if mask is not None:
                s = jnp.where(mask, s, -jnp.inf)
            m = jnp.maximum(jnp.max(s, axis=-1, keepdims=True), sink)
            p = jnp.exp(s - m)
            den = jnp.sum(p, axis=-1, keepdims=True) + jnp.exp(sink - m)
            halves.append(_dot(p, vdup[g]) / den)
        outs.append(jnp.where(lo_lanes, halves[0], halves[1]))
    return jnp.concatenate(outs, axis=1)


def _swa_banded_kernel(sink_ref, q_ref, kc_ref, kp_ref, vc_ref, vp_ref, o_ref):
    n = pl.program_id(1)
    kcat = jnp.concatenate([kp_ref[...], kc_ref[...]], axis=0)
    vcat = jnp.concatenate([vp_ref[...], vc_ref[...]], axis=0)
    qi = _iota((WINDOW, 2 * WINDOW), 0)
    kj = _iota((WINDOW, 2 * WINDOW), 1)
    diff = qi - kj + WINDOW
    mask = (diff >= 0) & (diff < WINDOW) & ((kj >= WINDOW) | (n > 0))
    o_ref[...] = _sink_attention(q_ref[...], _dup_kv(kcat), _dup_kv(vcat), sink_ref, mask)


def _swa_banded(q, k, v, sinks, nb):
    n = q.shape[0]
    nblk = n // nb // WINDOW
    cur = lambda w: pl.BlockSpec((WINDOW, w), lambda b, i: (b * nblk + i, 0))
    prev = lambda w: pl.BlockSpec((WINDOW, w), lambda b, i: (b * nblk + jnp.maximum(i - 1, 0), 0))
    return pl.pallas_call(
        _swa_banded_kernel, grid=(nb, nblk),
        in_specs=[pl.BlockSpec(memory_space=pltpu.SMEM), cur(B_WIDTH), cur(KV_WIDTH), prev(KV_WIDTH),
                  cur(KV_WIDTH), prev(KV_WIDTH)],
        out_specs=cur(B_WIDTH),
        out_shape=jax.ShapeDtypeStruct((n, B_WIDTH), F32),
        compiler_params=_params(("parallel", "parallel")), name="swa_banded",
    )(sinks, q, k, k, v, v)


def _swa_step_kernel(sink_ref, q_ref, k_ref, v_ref, kb_ref, vb_ref, o_ref, ko_ref, vo_ref, *, bb):
    last = _iota((WINDOW, 1), 0) == WINDOW - 1

    def body(bi, carry):
        row = pl.ds(bi, 1)
        kc = jnp.where(last, k_ref[row, :], pltpu.roll(kb_ref[bi], WINDOW - 1, 0))
        vc = jnp.where(last, v_ref[row, :], pltpu.roll(vb_ref[bi], WINDOW - 1, 0))
        ko_ref[bi] = kc
        vo_ref[bi] = vc
        q8 = jnp.broadcast_to(q_ref[row, :], (8, B_WIDTH))
        o = _sink_attention(q8, _dup_kv(kc), _dup_kv(vc), sink_ref, None)
        o_ref[row, :] = o[0:1, :]
        return carry

    lax.fori_loop(0, bb, body, 0)


def _swa_step(q, k, v, kbuf, vbuf, sinks, bb):
    n = q.shape[0]
    row = lambda w: pl.BlockSpec((bb, w), lambda i: (i, 0))
    buf = pl.BlockSpec((bb, WINDOW, KV_WIDTH), lambda i: (i, 0, 0))
    return pl.pallas_call(
        functools.partial(_swa_step_kernel, bb=bb), grid=(n // bb,),
        in_specs=[pl.BlockSpec(memory_space=pltpu.SMEM), row(B_WIDTH), row(KV_WIDTH), row(KV_WIDTH), buf, buf],
        out_specs=[row(B_WIDTH), buf, buf],
        out_shape=[jax.ShapeDtypeStruct((n, B_WIDTH), F32), jax.ShapeDtypeStruct(kbuf.shape, F32),
                   jax.ShapeDtypeStruct(vbuf.shape, F32)],
        compiler_params=_params(("parallel",)), name="swa_step",
    )(sinks, q, k, v, kbuf, vbuf)


def _outproj_kernel(x_ref, ya_ref, yb_ref, w_ref, o_ref):
    o_ref[...] = (x_ref[...] + _dot(ya_ref[...].astype(BF16), w_ref[0:A_WIDTH, :])
                  + _dot(yb_ref[...].astype(BF16), w_ref[A_WIDTH:, :]))


def _out_proj(x, ya, yb, w_out, tm):
    n = x.shape[0]
    row = lambda w: pl.BlockSpec((tm, w), lambda i: (i, 0))
    return pl.pallas_call(
        _outproj_kernel, grid=(n // tm,),
        in_specs=[row(D_MODEL), row(A_WIDTH), row(B_WIDTH), pl.BlockSpec(w_out.shape, lambda i: (0, 0))],
        out_specs=row(D_MODEL), out_shape=jax.ShapeDtypeStruct((n, D_MODEL), F32),
        compiler_params=_params(("parallel",)), name="out_proj",
    )(x, ya, yb, w_out)


def _ffn_kernel(x_ref, g_ref, wg_ref, wu_ref, wd_ref, fg_ref, o_ref, h_s, acc_s, *, nf, final):
    f = pl.program_id(1)

    @pl.when(f == 0)
    def _():
        x = x_ref[...]
        h_s[...] = _rms(x, g_ref[...]).astype(BF16)
        acc_s[...] = x

    h = h_s[...]
    gate = _dot(h, wg_ref[...])
    up = _dot(h, wu_ref[...])
    act = (gate * jax.nn.sigmoid(gate) * up).astype(BF16)
    acc_s[...] += _dot(act, wd_ref[...])

    @pl.when(f == nf - 1)
    def _():
        y = acc_s[...]
        o_ref[...] = _rms(y, fg_ref[...]) if final else y


def _ffn(x, gamma, w_gate, w_up, w_down, final_gamma, final, tm, tf):
    n = x.shape[0]
    nf = D_FF // tf
    row = pl.BlockSpec((tm, D_MODEL), lambda i, f: (i, 0))
    vec = pl.BlockSpec((1, D_MODEL), lambda i, f: (0, 0))
    return pl.pallas_call(
        functools.partial(_ffn_kernel, nf=nf, final=final), grid=(n // tm, nf),
        in_specs=[row, vec, pl.BlockSpec((D_MODEL, tf), lambda i, f: (0, f)),
                  pl.BlockSpec((D_MODEL, tf), lambda i, f: (0, f)),
                  pl.BlockSpec((tf, D_MODEL), lambda i, f: (f, 0)), vec],
        out_specs=row, out_shape=jax.ShapeDtypeStruct((n, D_MODEL), F32),
        scratch_shapes=[pltpu.VMEM((tm, D_MODEL), BF16), pltpu.VMEM((tm, D_MODEL), F32)],
        compiler_params=_params(("parallel", "arbitrary")), name="ffn",
    )(x, gamma, w_gate, w_up, w_down, final_gamma)


def _pad_lora_cols(t):
    z = lambda n: jnp.zeros(t.shape[:-1] + (n,), t.dtype)
    return jnp.concatenate([t[..., 0:64], z(64), t[..., 64:128], z(64), t[..., 128:288], z(96)], axis=-1)


def _unpad_lora_cols(t):
    return jnp.concatenate([t[..., 0:64], t[..., LORA_A_OFF:LORA_A_OFF + 64],
                            t[..., LORA_G_OFF:LORA_G_OFF + GATE_LORA]], axis=-1)


def _pad_rows(t, rows):
    return jnp.concatenate([t, jnp.zeros((rows - t.shape[0],) + t.shape[1:], t.dtype)], axis=0)


def _rope_tables(positions):
    half = HEAD_DIM // 2
    inv = ROPE_THETA ** (-jnp.arange(half, dtype=F32) / half)
    ang = positions.astype(F32)[:, None] * inv[None, :]
    cos = jnp.cos(ang)
    sin = jnp.sin(ang)
    return jnp.tile(cos, (1, LANES // half)), jnp.tile(jnp.concatenate([-sin, sin], axis=1), (1, LANES // HEAD_DIM))


def _layer_weights(l, attn_norm, w_in, mu, w0, w_decay_up, a0, w_a_up, w_g_up, k_k, k_a, r_k, lnx_g, lnx_b,
                   sinks, w_out, ffn_norm, w_gate, w_up, w_down):
    row = lambda t: t.reshape(1, -1)
    wi = w_in[l]
    w_all = jnp.concatenate([wi[:, 0:RKV_W], _pad_lora_cols(wi[:, RKV_W:A_PROJ]), wi[:, A_PROJ:]], axis=1)
    return dict(
        attn_norm=row(attn_norm[l]), w_all=w_all.astype(BF16),
        mu_rkv=row(mu[l, 0:RKV_W]), mu_lo=row(_pad_lora_cols(mu[l, RKV_W:])),
        w0=row(w0[l]), wd=_pad_rows(w_decay_up[l], LORA_A_OFF), a0=row(a0[l]),
        wa=_pad_rows(w_a_up[l], LORA_G_OFF - LORA_A_OFF), wg=_pad_rows(w_g_up[l], LORA_W - LORA_G_OFF),
        k_k=row(k_k[l]), k_a=row(k_a[l]), r_k=row(r_k[l]), lnx_g=row(lnx_g[l]), lnx_b=row(lnx_b[l]),
        sinks=sinks[l], w_out=w_out[l].astype(BF16), ffn_norm=row(ffn_norm[l]),
        w_gate=w_gate[l].astype(BF16), w_up=w_up[l].astype(BF16), w_down=w_down[l].astype(BF16))


def _last_pa_row(rkv, lo, nb):
    rkv_last = rkv.reshape(nb, -1, RKV_W)[:, -1]
    lo_last = lo.reshape(nb, -1, LORA_W)[:, -1]
    return jnp.concatenate([rkv_last, _unpad_lora_cols(lo_last)], axis=-1)


def kernel(x_prompt, x_sample, state_rwkv, state_shift, cache_k_win, cache_v_win, attn_norm, w_in, mu, w0, w_decay_up, a0, w_a_up, w_g_up, k_k, k_a, r_k, lnx_g, lnx_b, sinks, w_out, ffn_norm, w_gate, w_up, w_down, final_norm):
    bp, tp, _ = x_prompt.shape
    bs, ts, _ = x_sample.shape
    depth = w_in.shape[0]
    assert ts == 1, "sample kernels handle exactly one new token per sequence"
    tm_p = 512 if tp % 512 == 0 else tp
    tc = 256 if tp % 256 == 0 else tp
    tf = D_FF // 2
    sample_bb = 8

    cos_p, sin_p = _rope_tables(jnp.arange(tp, dtype=jnp.int32))
    cos_s, sin_s = _rope_tables(jnp.full((bs,), PAST_LEN, dtype=jnp.int32))
    final_g = final_norm.reshape(1, -1)

    xp = x_prompt.reshape(bp * tp, D_MODEL)
    xs = x_sample.reshape(bs * ts, D_MODEL)
    p_S, p_sh, p_k, p_v, s_S, s_sh, s_k, s_v = [], [], [], [], [], [], [], []
    for l in range(depth):
        wl = _layer_weights(l, attn_norm, w_in, mu, w0, w_decay_up, a0, w_a_up, w_g_up, k_k, k_a, r_k, lnx_g,
                            lnx_b, sinks, w_out, ffn_norm, w_gate, w_up, w_down)
        final = l == depth - 1

        rkv, lo, q, k, v = _in_proj(xp, wl["attn_norm"], wl["w_all"], cos_p, sin_p, tm_p)
        ya, S = _rwkv_chunk(rkv, lo, wl, bp, tc)
        yb = _swa_banded(q, k, v, wl["sinks"], bp)
        xp = _out_proj(xp, ya, yb, wl["w_out"], tm_p)
        xp = _ffn(xp, wl["ffn_norm"], wl["w_gate"], wl["w_up"], wl["w_down"], final_g, final, tm_p, tf)
        p_S.append(S)
        p_sh.append(_last_pa_row(rkv, lo, bp))
        p_k.append(k.reshape(bp, tp, N_KV_HEADS, HEAD_DIM)[:, tp - WINDOW:])
        p_v.append(v.reshape(bp, tp, N_KV_HEADS, HEAD_DIM)[:, tp - WINDOW:])

        rkv, lo, q, k, v = _in_proj(xs, wl["attn_norm"], wl["w_all"], cos_s, sin_s, bs)
        prev = state_shift[l]
        ya, S = _rwkv_step(rkv, lo, prev[:, 0:RKV_W], _pad_lora_cols(prev[:, RKV_W:]), state_rwkv[l], wl, sample_bb)
        yb, kwin, vwin = _swa_step(q, k, v, cache_k_win[l].reshape(bs, WINDOW, KV_WIDTH),
                                   cache_v_win[l].reshape(bs, WINDOW, KV_WIDTH), wl["sinks"], sample_bb)
        xs = _out_proj(xs, ya, yb, wl["w_out"], bs)
        xs = _ffn(xs, wl["ffn_norm"], wl["w_gate"], wl["w_up"], wl["w_down"], final_g, final, bs, tf)
        s_S.append(S)
        s_sh.append(_last_pa_row(rkv, lo, bs))
        s_k.append(kwin.reshape(bs, WINDOW, N_KV_HEADS, HEAD_DIM))
        s_v.append(vwin.reshape(bs, WINDOW, N_KV_HEADS, HEAD_DIM))

    return (xp.reshape(bp, tp, D_MODEL), xs.reshape(bs, ts, D_MODEL),
            jnp.stack(p_S), jnp.stack(p_sh), jnp.stack(p_k), jnp.stack(p_v),
            jnp.stack(s_S), jnp.stack(s_sh), jnp.stack(s_k), jnp.stack(s_v))
```

```python
import functools

import jax
import jax.numpy as jnp
from jax import lax
from jax.experimental import pallas as pl
from jax.experimental.pallas import tpu as pltpu

F32 = jnp.float32
BF16 = jnp.bfloat16

D_MODEL = 1024
HEAD_DIM = 64
N_A_HEADS = 8
A_WIDTH = N_A_HEADS * HEAD_DIM
N_Q_HEADS = 8
N_KV_HEADS = 2
GQA_GROUP = N_Q_HEADS // N_KV_HEADS
B_WIDTH = N_Q_HEADS * HEAD_DIM
KV_WIDTH = N_KV_HEADS * HEAD_DIM
DECAY_LORA = 64
AAA_LORA = 64
GATE_LORA = 160
A_PROJ = 3 * A_WIDTH + DECAY_LORA + AAA_LORA + GATE_LORA
WINDOW = 128
PAST_LEN = 8192
D_FF = 2816
ROPE_THETA = 10000.0
NORM_EPS = 1e-5
LNX_EPS = 64e-5
ATTN_SCALE = HEAD_DIM ** -0.5

LANES = 128
HEAD_PAIRS = N_A_HEADS // 2
CHUNK = 64
RKV_W = 3 * A_WIDTH
LORA_W = 512
LORA_A_OFF = 128
LORA_G_OFF = 256
PROJ_W = RKV_W + LORA_W + B_WIDTH + 2 * KV_WIDTH
VMEM_LIMIT = 48 * 1024 * 1024

_HI = lax.Precision.HIGHEST


def _dot(a, b, precision=None):
    return jnp.dot(a, b, precision=precision, preferred_element_type=F32)


def _dot_nt(a, b, precision=None):
    return lax.dot_general(a, b, (((1,), (1,)), ((), ())), precision=precision,
                           preferred_element_type=F32)


def _iota(shape, dim):
    return lax.broadcasted_iota(jnp.int32, shape, dim)


def _rms(x, g):
    return x * lax.rsqrt(jnp.mean(x * x, axis=-1, keepdims=True) + NORM_EPS) * g


def _bdot(a, b):
    return jnp.dot(a.astype(BF16), b.astype(BF16), preferred_element_type=F32)


def _bdot_nt(a, b):
    return lax.dot_general(a.astype(BF16), b.astype(BF16), (((1,), (1,)), ((), ())), preferred_element_type=F32)


def _split_bf16(x, parts):
    out = []
    for _ in range(parts):
        piece = x.astype(BF16)
        out.append(piece)
        x = x - piece.astype(F32)
    return out


def _head_sum(x):
    ones = (_iota((LANES, LANES), 0) // HEAD_DIM == _iota((LANES, LANES), 1) // HEAD_DIM).astype(BF16)
    hi, lo = _split_bf16(x, 2)
    return _dot(hi, ones) + _dot(lo, ones)


def _params(sem):
    return pltpu.CompilerParams(dimension_semantics=sem, vmem_limit_bytes=VMEM_LIMIT)


def _inproj_kernel(x_ref, g_ref, w_ref, cos_ref, sin_ref, rkv_ref, lo_ref, q_ref, k_ref, v_ref):
    hb = _rms(x_ref[...], g_ref[...]).astype(BF16)
    rkv_ref[...] = _dot(hb, w_ref[:, 0:RKV_W])
    lo_ref[...] = _dot(hb, w_ref[:, RKV_W:RKV_W + LORA_W])
    qkv = _dot(hb, w_ref[:, RKV_W + LORA_W:PROJ_W])
    cos = cos_ref[...]
    sin = sin_ref[...]
    first_half = (_iota((1, LANES), 1) % HEAD_DIM) < (HEAD_DIM // 2)

    def rope(t):
        rot = jnp.where(first_half, pltpu.roll(t, LANES - HEAD_DIM // 2, 1), pltpu.roll(t, HEAD_DIM // 2, 1))
        return t * cos + rot * sin

    for j in range(B_WIDTH // LANES):
        q_ref[:, j * LANES:(j + 1) * LANES] = rope(qkv[:, j * LANES:(j + 1) * LANES])
    k_ref[...] = rope(qkv[:, B_WIDTH:B_WIDTH + KV_WIDTH])
    v_ref[...] = qkv[:, B_WIDTH + KV_WIDTH:]


def _in_proj(x, gamma, w_all, cos, sin, tm):
    n = x.shape[0]
    nrope = cos.shape[0] // tm
    row = lambda w: pl.BlockSpec((tm, w), lambda i: (i, 0))
    rope_row = pl.BlockSpec((tm, LANES), lambda i: (i % nrope, 0))
    const = lambda a: pl.BlockSpec(a.shape, lambda i: (0,) * a.ndim)
    sds = lambda w: jax.ShapeDtypeStruct((n, w), F32)
    return pl.pallas_call(
        _inproj_kernel, grid=(n // tm,),
        in_specs=[row(D_MODEL), const(gamma), const(w_all), rope_row, rope_row],
        out_specs=[row(RKV_W), row(LORA_W), row(B_WIDTH), row(KV_WIDTH), row(KV_WIDTH)],
        out_shape=[sds(RKV_W), sds(LORA_W), sds(B_WIDTH), sds(KV_WIDTH), sds(KV_WIDTH)],
        compiler_params=_params(("parallel",)), name="in_proj",
    )(x, gamma, w_all, cos, sin)


def _rwkv_prep(xs_rkv, xs_lo, w0, wd, a0, wa, wg, k_k, k_a):
    r = xs_rkv[:, 0:A_WIDTH]
    k = xs_rkv[:, A_WIDTH:2 * A_WIDTH]
    v = xs_rkv[:, 2 * A_WIDTH:3 * A_WIDTH]
    z = w0 + _bdot(jnp.tanh(xs_lo[:, 0:LORA_A_OFF]), wd)
    nz = -z
    softplus = jnp.maximum(nz, 0.0) + jnp.log1p(jnp.exp(-jnp.abs(nz)))
    logw = -jnp.exp(-softplus - 0.5)
    a = jax.nn.sigmoid(a0 + _bdot(xs_lo[:, LORA_A_OFF:LORA_G_OFF], wa))
    g = _bdot(jax.nn.sigmoid(xs_lo[:, LORA_G_OFF:LORA_W]), wg)
    kk = k * k_k
    parts = []
    for j in range(A_WIDTH // LANES):
        t = kk[:, j * LANES:(j + 1) * LANES]
        nrm = jnp.sqrt(_head_sum(t * t))
        parts.append(t / jnp.maximum(nrm, 1e-12))
    kk = jnp.concatenate(parts, axis=1)
    k_mod = k * (1.0 + (a - 1.0) * k_a)
    return r, logw, k_mod, v, -kk, kk * a, g


def _rwkv_post(y, r, k_mod, v, g, r_k, lnx_g, lnx_b):
    parts = []
    for j in range(A_WIDTH // LANES):
        ln = slice(j * LANES, (j + 1) * LANES)
        yj = y[:, ln]
        mean = _head_sum(yj) * (1.0 / HEAD_DIM)
        d = yj - mean
        var = _head_sum(d * d) * (1.0 / HEAD_DIM)
        yn = d * lax.rsqrt(var + LNX_EPS) * lnx_g[:, ln] + lnx_b[:, ln]
        bonus = _head_sum(r[:, ln] * k_mod[:, ln] * r_k[:, ln]) * v[:, ln]
        parts.append((yn + bonus) * g[:, ln])
    return jnp.concatenate(parts, axis=1)


def _rwkv_chunk_kernel(rkv_ref, lo_ref, mu_rkv_ref, mu_lo_ref, w0_ref, wd_ref, a0_ref, wa_ref, wg_ref,
                       kk_ref, ka_ref, rk_ref, lng_ref, lnb_ref,
                       ya_ref, s_out_ref,
                       s_scr, carry_rkv, carry_lo, r_s, lw_s, km_s, v_s, an_s, bn_s, g_s, y_s, cum_s, *, tc, nt):
    t = pl.program_id(1)

    @pl.when(t == 0)
    def _():
        s_scr[...] = jnp.zeros_like(s_scr)
        carry_rkv[...] = jnp.zeros_like(carry_rkv)
        carry_lo[...] = jnp.zeros_like(carry_lo)

    rkv = rkv_ref[...]
    lo = lo_ref[...]
    row0 = _iota((tc, 1), 0) == 0
    prev_rkv = jnp.where(row0, carry_rkv[7:8, :], pltpu.roll(rkv, 1, 0))
    prev_lo = jnp.where(row0, carry_lo[7:8, :], pltpu.roll(lo, 1, 0))
    carry_rkv[...] = rkv[tc - 8:tc, :]
    carry_lo[...] = lo[tc - 8:tc, :]
    xs_rkv = rkv + mu_rkv_ref[...] * (prev_rkv - rkv)
    xs_lo = lo + mu_lo_ref[...] * (prev_lo - lo)
    r, logw, k_mod, v, a_neg, b_pos, g = _rwkv_prep(
        xs_rkv, xs_lo, w0_ref[...], wd_ref[...], a0_ref[...], wa_ref[...], wg_ref[...], kk_ref[...], ka_ref[...])
    r_s[...] = r
    lw_s[...] = logw
    km_s[...] = k_mod
    v_s[...] = v
    an_s[...] = a_neg
    bn_s[...] = b_pos
    g_s[...] = g
    ti = _iota((tc, tc), 0)
    tj = _iota((tc, tc), 1)
    tri = ((ti // CHUNK == tj // CHUNK) & (ti >= tj)).astype(BF16)
    cum_s[...] = sum(_dot(tri, piece) for piece in _split_bf16(logw, 3))

    ri = _iota((LANES, LANES), 0)
    rj = _iota((LANES, LANES), 1)
    same = (ri // CHUNK) == (rj // CHUNK)
    strict = same & (ri > rj)
    incl = same & (ri >= rj)
    eye = (ri == rj).astype(F32)
    lo_lanes = _iota((1, LANES), 1) < HEAD_DIM

    def stack(x):
        return jnp.concatenate([jnp.where(lo_lanes, x, 0.0), jnp.where(lo_lanes, 0.0, x)], axis=0)

    def chunk(c, carry):
        rows = pl.ds(pl.multiple_of(c * CHUNK, CHUNK), CHUNK)
        pairs = range(HEAD_PAIRS)
        lanes = [slice(p * LANES, (p + 1) * LANES) for p in pairs]
        ar, bk_end, v2, w_end, gram = [], [], [], [], []
        for p in pairs:
            ln = lanes[p]
            lw = lw_s[rows, ln]
            cum = cum_s[rows, ln]
            w_inv = jnp.exp(-cum)
            w_end.append(jnp.exp(cum[CHUNK - 1:CHUNK, :]))
            a2 = stack(an_s[rows, ln] * jnp.exp(cum - lw))
            r2 = stack(r_s[rows, ln] * jnp.exp(cum))
            b2 = stack(bn_s[rows, ln] * w_inv)
            k2 = stack(km_s[rows, ln] * w_inv)
            v2.append(stack(v_s[rows, ln]).astype(BF16))
            ar.append(jnp.concatenate([a2, r2], axis=0).astype(BF16))
            bk = jnp.concatenate([b2, k2], axis=0)
            bk_end.append((bk * w_end[p]).astype(BF16))
            gram.append(_bdot_nt(ar[p], bk))
        a_ab = [jnp.where(strict, gram[p][0:LANES, 0:LANES], 0.0) for p in pairs]
        a_ak = [jnp.where(strict, gram[p][0:LANES, LANES:], 0.0) for p in pairs]
        incl2 = jnp.concatenate([incl, incl], axis=1)
        q_bk = [jnp.where(incl2, gram[p][LANES:, :], 0.0) for p in pairs]
        npow = [_bdot(a_ab[p], a_ab[p]) for p in pairs]
        tinv = [eye + a_ab[p] for p in pairs]
        for _ in range(4):
            both = [_bdot(jnp.concatenate([npow[p], tinv[p]], axis=0), npow[p]) for p in pairs]
            npow = [both[p][0:LANES, :] for p in pairs]
            tinv = [tinv[p] + both[p][LANES:, :] for p in pairs]
        tinv = [tinv[p] + _bdot(tinv[p], npow[p]) for p in pairs]
        akv = [_bdot(a_ak[p], v2[p]) for p in pairs]
        s2 = [s_scr[p] for p in pairs]
        ars = [_bdot_nt(ar[p], s2[p]) for p in pairs]
        u2 = [_bdot(tinv[p], ars[p][0:LANES, :] + akv[p]) for p in pairs]
        uv = [jnp.concatenate([u2[p].astype(BF16), v2[p]], axis=0) for p in pairs]
        y2 = [ars[p][LANES:, :] + _bdot(q_bk[p], uv[p]) for p in pairs]
        uv_t = [jnp.concatenate([u2[p].T, v2[p].astype(F32).T], axis=1) for p in pairs]
        s_new = [s2[p] * w_end[p] + _bdot(uv_t[p], bk_end[p]) for p in pairs]
        for p in pairs:
            y_s[rows, lanes[p]] = y2[p][0:CHUNK, :] + y2[p][CHUNK:, :]
            s_scr[p] = s_new[p]
        return carry

    lax.fori_loop(0, tc // CHUNK, chunk, 0)

    ya_ref[...] = _rwkv_post(y_s[...], r_s[...], km_s[...], v_s[...], g_s[...],
                             rk_ref[...], lng_ref[...], lnb_ref[...])

    @pl.when(t == nt - 1)
    def _():
        for p in range(HEAD_PAIRS):
            s2 = s_scr[p]
            s_out_ref[0, 2 * p] = s2[0:HEAD_DIM, 0:HEAD_DIM]
            s_out_ref[0, 2 * p + 1] = s2[HEAD_DIM:, HEAD_DIM:]


def _rwkv_chunk(rkv, lo, wl, nb, tc):
    n = rkv.shape[0]
    nt = n // nb // tc
    row = lambda w: pl.BlockSpec((tc, w), lambda b, t: (b * nt + t, 0))
    const = lambda a: pl.BlockSpec(a.shape, lambda b, t: (0,) * a.ndim)
    consts = [wl[k] for k in ("mu_rkv", "mu_lo", "w0", "wd", "a0", "wa", "wg", "k_k", "k_a", "r_k", "lnx_g", "lnx_b")]
    wide = lambda: pltpu.VMEM((tc, A_WIDTH), F32)
    return pl.pallas_call(
        functools.partial(_rwkv_chunk_kernel, tc=tc, nt=nt), grid=(nb, nt),
        in_specs=[row(RKV_W), row(LORA_W)] + [const(a) for a in consts],
        out_specs=[row(A_WIDTH), pl.BlockSpec((1, N_A_HEADS, HEAD_DIM, HEAD_DIM), lambda b, t: (b, 0, 0, 0))],
        out_shape=[jax.ShapeDtypeStruct((n, A_WIDTH), F32),
                   jax.ShapeDtypeStruct((nb, N_A_HEADS, HEAD_DIM, HEAD_DIM), F32)],
        scratch_shapes=[pltpu.VMEM((HEAD_PAIRS, LANES, LANES), F32), pltpu.VMEM((8, RKV_W), F32),
                        pltpu.VMEM((8, LORA_W), F32)] + [wide() for _ in range(9)],
        compiler_params=_params(("arbitrary", "arbitrary")), name="rwkv_chunk",
    )(rkv, lo, *consts)


def _rwkv_step_kernel(rkv_ref, lo_ref, prkv_ref, plo_ref, s_ref, mu_rkv_ref, mu_lo_ref, w0_ref, wd_ref, a0_ref,
                      wa_ref, wg_ref, kk_ref, ka_ref, rk_ref, lng_ref, lnb_ref,
                      ya_ref, s_out_ref,
                      r_s, w_s, km_s, v_s, an_s, bn_s, y_s, *, bb):
    rkv = rkv_ref[...]
    lo = lo_ref[...]
    xs_rkv = rkv + mu_rkv_ref[...] * (prkv_ref[...] - rkv)
    xs_lo = lo + mu_lo_ref[...] * (plo_ref[...] - lo)
    r, logw, k_mod, v, a_neg, b_pos, g = _rwkv_prep(
        xs_rkv, xs_lo, w0_ref[...], wd_ref[...], a0_ref[...], wa_ref[...], wg_ref[...], kk_ref[...], ka_ref[...])
    r_s[...] = r
    w_s[...] = jnp.exp(logw)
    km_s[...] = k_mod
    v_s[...] = v
    an_s[...] = a_neg
    bn_s[...] = b_pos
    eye = _iota((HEAD_DIM, HEAD_DIM), 0) == _iota((HEAD_DIM, HEAD_DIM), 1)

    for bi in range(bb):
        row = slice(bi, bi + 1)
        for h in range(N_A_HEADS):
            ln = slice(h * HEAD_DIM, (h + 1) * HEAD_DIM)
            s = s_ref[bi, h]
            sa = jnp.sum(s * an_s[row, ln], axis=-1, keepdims=True)
            v_col = jnp.sum(jnp.where(eye, v_s[row, ln], 0.0), axis=-1, keepdims=True)
            s = s * w_s[row, ln] + sa * bn_s[row, ln] + v_col * km_s[row, ln]
            s_out_ref[bi, h] = s
            y_col = jnp.sum(s * r_s[row, ln], axis=-1, keepdims=True)
            y_s[row, ln] = jnp.sum(jnp.where(eye, y_col, 0.0), axis=0, keepdims=True)
    ya_ref[...] = _rwkv_post(y_s[...], r, k_mod, v, g, rk_ref[...], lng_ref[...], lnb_ref[...])


def _rwkv_step(rkv, lo, prev_rkv, prev_lo, state, wl, bb):
    n = rkv.shape[0]
    row = lambda w: pl.BlockSpec((bb, w), lambda i: (i, 0))
    st = pl.BlockSpec((bb, N_A_HEADS, HEAD_DIM, HEAD_DIM), lambda i: (i, 0, 0, 0))
    const = lambda a: pl.BlockSpec(a.shape, lambda i: (0,) * a.ndim)
    consts = [wl[k] for k in ("mu_rkv", "mu_lo", "w0", "wd", "a0", "wa", "wg", "k_k", "k_a", "r_k", "lnx_g", "lnx_b")]
    return pl.pallas_call(
        functools.partial(_rwkv_step_kernel, bb=bb), grid=(n // bb,),
        in_specs=[row(RKV_W), row(LORA_W), row(RKV_W), row(LORA_W), st] + [const(a) for a in consts],
        out_specs=[row(A_WIDTH), st],
        out_shape=[jax.ShapeDtypeStruct((n, A_WIDTH), F32), jax.ShapeDtypeStruct(state.shape, F32)],
        scratch_shapes=[pltpu.VMEM((bb, A_WIDTH), F32) for _ in range(7)],
        compiler_params=_params(("parallel",)), name="rwkv_step",
    )(rkv, lo, prev_rkv, prev_lo, state, *consts)


def _dup_kv(x):
    lo_lanes = _iota((1, LANES), 1) < HEAD_DIM
    xr = pltpu.roll(x, HEAD_DIM, 1)
    return [jnp.where(lo_lanes, x, xr), jnp.where(lo_lanes, xr, x)]


def _sink_attention(q, kdup, vdup, sink_ref, mask):
    lo_lanes = _iota((1, LANES), 1) < HEAD_DIM
    outs = []
    for j in range(B_WIDTH // LANES):
        g = (2 * j) // GQA_GROUP
        q128 = q[:, j * LANES:(j + 1) * LANES]
        halves = []
        for e in range(2):
            sink = sink_ref[2 * j + e]
            qm = jnp.where(lo_lanes, q128, 0.0) if e == 0 else jnp.where(lo_lanes, 0.0, q128)
            s = _dot_nt(qm, kdup[g]) * ATTN_SCALE
            if mask is not None:
                s = jnp.where(mask, s, -jnp.inf)
            m = jnp.maximum(jnp.max(s, axis=-1, keepdims=True), sink)
            p = jnp.exp(s - m)
            den = jnp.sum(p, axis=-1, keepdims=True) + jnp.exp(sink - m)
            halves.append(_dot(p, vdup[g]) / den)
        outs.append(jnp.where(lo_lanes, halves[0], halves[1]))
    return jnp.concatenate(outs, axis=1)


def _swa_banded_kernel(sink_ref, q_ref, kc_ref, kp_ref, vc_ref, vp_ref, o_ref):
    n = pl.program_id(1)
    kcat = jnp.concatenate([kp_ref[...], kc_ref[...]], axis=0)
    vcat = jnp.concatenate([vp_ref[...], vc_ref[...]], axis=0)
    qi = _iota((WINDOW, 2 * WINDOW), 0)
    kj = _iota((WINDOW, 2 * WINDOW), 1)
    diff = qi - kj + WINDOW
    mask = (diff >= 0) & (diff < WINDOW) & ((kj >= WINDOW) | (n > 0))
    o_ref[...] = _sink_attention(q_ref[...], _dup_kv(kcat), _dup_kv(vcat), sink_ref, mask)


def _swa_banded(q, k, v, sinks, nb):
    n = q.shape[0]
    nblk = n // nb // WINDOW
    cur = lambda w: pl.BlockSpec((WINDOW, w), lambda b, i: (b * nblk + i, 0))
    prev = lambda w: pl.BlockSpec((WINDOW, w), lambda b, i: (b * nblk + jnp.maximum(i - 1, 0), 0))
    return pl.pallas_call(
        _swa_banded_kernel, grid=(nb, nblk),
        in_specs=[pl.BlockSpec(memory_space=pltpu.SMEM), cur(B_WIDTH), cur(KV_WIDTH), prev(KV_WIDTH),
                  cur(KV_WIDTH), prev(KV_WIDTH)],
        out_specs=cur(B_WIDTH),
        out_shape=jax.ShapeDtypeStruct((n, B_WIDTH), F32),
        compiler_params=_params(("parallel", "parallel")), name="swa_banded",
    )(sinks, q, k, k, v, v)


def _swa_step_kernel(sink_ref, q_ref, k_ref, v_ref, kb_ref, vb_ref, o_ref, ko_ref, vo_ref, *, bb):
    last = _iota((WINDOW, 1), 0) == WINDOW - 1

    def body(bi, carry):
        row = pl.ds(bi, 1)
        kc = jnp.where(last, k_ref[row, :], pltpu.roll(kb_ref[bi], WINDOW - 1, 0))
        vc = jnp.where(last, v_ref[row, :], pltpu.roll(vb_ref[bi], WINDOW - 1, 0))
        ko_ref[bi] = kc
        vo_ref[bi] = vc
        q8 = jnp.broadcast_to(q_ref[row, :], (8, B_WIDTH))
        o = _sink_attention(q8, _dup_kv(kc), _dup_kv(vc), sink_ref, None)
        o_ref[row, :] = o[0:1, :]
        return carry

    lax.fori_loop(0, bb, body, 0)


def _swa_step(q, k, v, kbuf, vbuf, sinks, bb):
    n = q.shape[0]
    row = lambda w: pl.BlockSpec((bb, w), lambda i: (i, 0))
    buf = pl.BlockSpec((bb, WINDOW, KV_WIDTH), lambda i: (i, 0, 0))
    return pl.pallas_call(
        functools.partial(_swa_step_kernel, bb=bb), grid=(n // bb,),
        in_specs=[pl.BlockSpec(memory_space=pltpu.SMEM), row(B_WIDTH), row(KV_WIDTH), row(KV_WIDTH), buf, buf],
        out_specs=[row(B_WIDTH), buf, buf],
        out_shape=[jax.ShapeDtypeStruct((n, B_WIDTH), F32), jax.ShapeDtypeStruct(kbuf.shape, F32),
                   jax.ShapeDtypeStruct(vbuf.shape, F32)],
        compiler_params=_params(("parallel",)), name="swa_step",
    )(sinks, q, k, v, kbuf, vbuf)


def _outproj_kernel(x_ref, ya_ref, yb_ref, w_ref, o_ref):
    o_ref[...] = (x_ref[...] + _dot(ya_ref[...].astype(BF16), w_ref[0:A_WIDTH, :])
                  + _dot(yb_ref[...].astype(BF16), w_ref[A_WIDTH:, :]))


def _out_proj(x, ya, yb, w_out, tm):
    n = x.shape[0]
    row = lambda w: pl.BlockSpec((tm, w), lambda i: (i, 0))
    return pl.pallas_call(
        _outproj_kernel, grid=(n // tm,),
        in_specs=[row(D_MODEL), row(A_WIDTH), row(B_WIDTH), pl.BlockSpec(w_out.shape, lambda i: (0, 0))],
        out_specs=row(D_MODEL), out_shape=jax.ShapeDtypeStruct((n, D_MODEL), F32),
        compiler_params=_params(("parallel",)), name="out_proj",
    )(x, ya, yb, w_out)


def _ffn_kernel(x_ref, g_ref, wg_ref, wu_ref, wd_ref, fg_ref, o_ref, h_s, acc_s, *, nf, final):
    f = pl.program_id(1)

    @pl.when(f == 0)
    def _():
        x = x_ref[...]
        h_s[...] = _rms(x, g_ref[...]).astype(BF16)
        acc_s[...] = x

    h = h_s[...]
    gate = _dot(h, wg_ref[...])
    up = _dot(h, wu_ref[...])
    act = (gate * jax.nn.sigmoid(gate) * up).astype(BF16)
    acc_s[...] += _dot(act, wd_ref[...])

    @pl.when(f == nf - 1)
    def _():
        y = acc_s[...]
        o_ref[...] = _rms(y, fg_ref[...]) if final else y


def _ffn(x, gamma, w_gate, w_up, w_down, final_gamma, final, tm, tf):
    n = x.shape[0]
    nf = D_FF // tf
    row = pl.BlockSpec((tm, D_MODEL), lambda i, f: (i, 0))
    vec = pl.BlockSpec((1, D_MODEL), lambda i, f: (0, 0))
    return pl.pallas_call(
        functools.partial(_ffn_kernel, nf=nf, final=final), grid=(n // tm, nf),
        in_specs=[row, vec, pl.BlockSpec((D_MODEL, tf), lambda i, f: (0, f)),
                  pl.BlockSpec((D_MODEL, tf), lambda i, f: (0, f)),
                  pl.BlockSpec((tf, D_MODEL), lambda i, f: (f, 0)), vec],
        out_specs=row, out_shape=jax.ShapeDtypeStruct((n, D_MODEL), F32),
        scratch_shapes=[pltpu.VMEM((tm, D_MODEL), BF16), pltpu.VMEM((tm, D_MODEL), F32)],
        compiler_params=_params(("parallel", "arbitrary")), name="ffn",
    )(x, gamma, w_gate, w_up, w_down, final_gamma)


def _pad_lora_cols(t):
    z = lambda n: jnp.zeros(t.shape[:-1] + (n,), t.dtype)
    return jnp.concatenate([t[..., 0:64], z(64), t[..., 64:128], z(64), t[..., 128:288], z(96)], axis=-1)


def _unpad_lora_cols(t):
    return jnp.concatenate([t[..., 0:64], t[..., LORA_A_OFF:LORA_A_OFF + 64],
                            t[..., LORA_G_OFF:LORA_G_OFF + GATE_LORA]], axis=-1)


def _pad_rows(t, rows):
    return jnp.concatenate([t, jnp.zeros((rows - t.shape[0],) + t.shape[1:], t.dtype)], axis=0)


def _rope_tables(positions):
    half = HEAD_DIM // 2
    inv = ROPE_THETA ** (-jnp.arange(half, dtype=F32) / half)
    ang = positions.astype(F32)[:, None] * inv[None, :]
    cos = jnp.cos(ang)
    sin = jnp.sin(ang)
    return jnp.tile(cos, (1, LANES // half)), jnp.tile(jnp.concatenate([-sin, sin], axis=1), (1, LANES // HEAD_DIM))


def _layer_weights(l, attn_norm, w_in, mu, w0, w_decay_up, a0, w_a_up, w_g_up, k_k, k_a, r_k, lnx_g, lnx_b,
                   sinks, w_out, ffn_norm, w_gate, w_up, w_down):
    row = lambda t: t.reshape(1, -1)
    wi = w_in[l]
    w_all = jnp.concatenate([wi[:, 0:RKV_W], _pad_lora_cols(wi[:, RKV_W:A_PROJ]), wi[:, A_PROJ:]], axis=1)
    return dict(
        attn_norm=row(attn_norm[l]), w_all=w_all.astype(BF16),
        mu_rkv=row(mu[l, 0:RKV_W]), mu_lo=row(_pad_lora_cols(mu[l, RKV_W:])),
        w0=row(w0[l]), wd=_pad_rows(w_decay_up[l], LORA_A_OFF), a0=row(a0[l]),
        wa=_pad_rows(w_a_up[l], LORA_G_OFF - LORA_A_OFF), wg=_pad_rows(w_g_up[l], LORA_W - LORA_G_OFF),
        k_k=row(k_k[l]), k_a=row(k_a[l]), r_k=row(r_k[l]), lnx_g=row(lnx_g[l]), lnx_b=row(lnx_b[l]),
        sinks=sinks[l], w_out=w_out[l].astype(BF16), ffn_norm=row(ffn_norm[l]),
        w_gate=w_gate[l].astype(BF16), w_up=w_up[l].astype(BF16), w_down=w_down[l].astype(BF16))


def _last_pa_row(rkv, lo, nb):
    rkv_last = rkv.reshape(nb, -1, RKV_W)[:, -1]
    lo_last = lo.reshape(nb, -1, LORA_W)[:, -1]
    return jnp.concatenate([rkv_last, _unpad_lora_cols(lo_last)], axis=-1)


def kernel(x_prompt, x_sample, state_rwkv, state_shift, cache_k_win, cache_v_win, attn_norm, w_in, mu, w0, w_decay_up, a0, w_a_up, w_g_up, k_k, k_a, r_k, lnx_g, lnx_b, sinks, w_out, ffn_norm, w_gate, w_up, w_down, final_norm):
    bp, tp, _ = x_prompt.shape
    bs, ts, _ = x_sample.shape
    depth = w_in.shape[0]
    assert ts == 1, "sample kernels handle exactly one new token per sequence"
    tm_p = 512 if tp % 512 == 0 else tp
    tc = 256 if tp % 256 == 0 else tp
    tf = D_FF // 2
    sample_bb = 8

    cos_p, sin_p = _rope_tables(jnp.arange(tp, dtype=jnp.int32))
    cos_s, sin_s = _rope_tables(jnp.full((bs,), PAST_LEN, dtype=jnp.int32))
    final_g = final_norm.reshape(1, -1)

    xp = x_prompt.reshape(bp * tp, D_MODEL)
    xs = x_sample.reshape(bs * ts, D_MODEL)
    p_S, p_sh, p_k, p_v, s_S, s_sh, s_k, s_v = [], [], [], [], [], [], [], []
    for l in range(depth):
        wl = _layer_weights(l, attn_norm, w_in, mu, w0, w_decay_up, a0, w_a_up, w_g_up, k_k, k_a, r_k, lnx_g,
                            lnx_b, sinks, w_out, ffn_norm, w_gate, w_up, w_down)
        final = l == depth - 1

        rkv, lo, q, k, v = _in_proj(xp, wl["attn_norm"], wl["w_all"], cos_p, sin_p, tm_p)
        ya, S = _rwkv_chunk(rkv, lo, wl, bp, tc)
        yb = _swa_banded(q, k, v, wl["sinks"], bp)
        xp = _out_proj(xp, ya, yb, wl["w_out"], tm_p)
        xp = _ffn(xp, wl["ffn_norm"], wl["w_gate"], wl["w_up"], wl["w_down"], final_g, final, tm_p, tf)
        p_S.append(S)
        p_sh.append(_last_pa_row(rkv, lo, bp))
        p_k.append(k.reshape(bp, tp, N_KV_HEADS, HEAD_DIM)[:, tp - WINDOW:])
        p_v.append(v.reshape(bp, tp, N_KV_HEADS, HEAD_DIM)[:, tp - WINDOW:])

        rkv, lo, q, k, v = _in_proj(xs, wl["attn_norm"], wl["w_all"], cos_s, sin_s, bs)
        prev = state_shift[l]
        ya, S = _rwkv_step(rkv, lo, prev[:, 0:RKV_W], _pad_lora_cols(prev[:, RKV_W:]), state_rwkv[l], wl, sample_bb)
        yb, kwin, vwin = _swa_step(q, k, v, cache_k_win[l].reshape(bs, WINDOW, KV_WIDTH),
                                   cache_v_win[l].reshape(bs, WINDOW, KV_WIDTH), wl["sinks"], sample_bb)
        xs = _out_proj(xs, ya, yb, wl["w_out"], bs)
        xs = _ffn(xs, wl["ffn_norm"], wl["w_gate"], wl["w_up"], wl["w_down"], final_g, final, bs, tf)
        s_S.append(S)
        s_sh.append(_last_pa_row(rkv, lo, bs))
        s_k.append(kwin.reshape(bs, WINDOW, N_KV_HEADS, HEAD_DIM))
        s_v.append(vwin.reshape(bs, WINDOW, N_KV_HEADS, HEAD_DIM))

    return (xp.reshape(bp, tp, D_MODEL), xs.reshape(bs, ts, D_MODEL),
            jnp.stack(p_S), jnp.stack(p_sh), jnp.stack(p_k), jnp.stack(p_v),
            jnp.stack(s_S), jnp.stack(s_sh), jnp.stack(s_k), jnp.stack(s_v))
```

```python
import functools

import jax
import jax.numpy as jnp
from jax import lax
from jax.experimental import pallas as pl
from jax.experimental.pallas import tpu as pltpu

F32 = jnp.float32
BF16 = jnp.bfloat16

D_MODEL = 1024
HEAD_DIM = 64
N_A_HEADS = 8
A_WIDTH = N_A_HEADS * HEAD_DIM
N_Q_HEADS = 8
N_KV_HEADS = 2
GQA_GROUP = N_Q_HEADS // N_KV_HEADS
B_WIDTH = N_Q_HEADS * HEAD_DIM
KV_WIDTH = N_KV_HEADS * HEAD_DIM
DECAY_LORA = 64
AAA_LORA = 64
GATE_LORA = 160
A_PROJ = 3 * A_WIDTH + DECAY_LORA + AAA_LORA + GATE_LORA
WINDOW = 128
PAST_LEN = 8192
D_FF = 2816
ROPE_THETA = 10000.0
NORM_EPS = 1e-5
LNX_EPS = 64e-5
ATTN_SCALE = HEAD_DIM ** -0.5
EXP_NEG_HALF = 0.6065306597126334

LANES = 128
HEAD_PAIRS = N_A_HEADS // 2
CHUNK = 64
RKV_W = 3 * A_WIDTH
LORA_W = 512
LORA_A_OFF = 128
LORA_G_OFF = 256
PROJ_W = RKV_W + LORA_W + B_WIDTH + 2 * KV_WIDTH
VMEM_LIMIT = 48 * 1024 * 1024

_HI = lax.Precision.HIGHEST


def _dot(a, b, precision=None):
    return jnp.dot(a, b, precision=precision, preferred_element_type=F32)


def _dot_nt(a, b, precision=None):
    return lax.dot_general(a, b, (((1,), (1,)), ((), ())), precision=precision,
                           preferred_element_type=F32)


def _iota(shape, dim):
    return lax.broadcasted_iota(jnp.int32, shape, dim)


def _rms(x, g):
    return x * lax.rsqrt(jnp.mean(x * x, axis=-1, keepdims=True) + NORM_EPS) * g


def _bdot(a, b):
    return jnp.dot(a.astype(BF16), b.astype(BF16), preferred_element_type=F32)


def _bdot_nt(a, b):
    return lax.dot_general(a.astype(BF16), b.astype(BF16), (((1,), (1,)), ((), ())), preferred_element_type=F32)


def _split_bf16(x, parts):
    out = []
    for _ in range(parts):
        piece = x.astype(BF16)
        out.append(piece)
        x = x - piece.astype(F32)
    return out


def _head_sum(x, parts):
    ones = (_iota((LANES, LANES), 0) // HEAD_DIM == _iota((LANES, LANES), 1) // HEAD_DIM).astype(BF16)
    return sum(_dot(piece, ones) for piece in _split_bf16(x, parts))


def _params(sem):
    return pltpu.CompilerParams(dimension_semantics=sem, vmem_limit_bytes=VMEM_LIMIT)


def _inproj_kernel(x_ref, g_ref, w_ref, cos_ref, sin_ref, rkv_ref, lo_ref, q_ref, k_ref, v_ref):
    hb = _rms(x_ref[...], g_ref[...]).astype(BF16)
    rkv_ref[...] = _dot(hb, w_ref[:, 0:RKV_W])
    lo_ref[...] = _dot(hb, w_ref[:, RKV_W:RKV_W + LORA_W])
    qkv = _dot(hb, w_ref[:, RKV_W + LORA_W:PROJ_W])
    cos = cos_ref[...]
    sin = sin_ref[...]
    first_half = (_iota((1, LANES), 1) % HEAD_DIM) < (HEAD_DIM // 2)

    def rope(t):
        rot = jnp.where(first_half, pltpu.roll(t, LANES - HEAD_DIM // 2, 1), pltpu.roll(t, HEAD_DIM // 2, 1))
        return t * cos + rot * sin

    for j in range(B_WIDTH // LANES):
        q_ref[:, j * LANES:(j + 1) * LANES] = rope(qkv[:, j * LANES:(j + 1) * LANES])
    k_ref[...] = rope(qkv[:, B_WIDTH:B_WIDTH + KV_WIDTH])
    v_ref[...] = qkv[:, B_WIDTH + KV_WIDTH:]


def _in_proj(x, gamma, w_all, cos, sin, tm):
    n = x.shape[0]
    nrope = cos.shape[0] // tm
    row = lambda w: pl.BlockSpec((tm, w), lambda i: (i, 0))
    rope_row = pl.BlockSpec((tm, LANES), lambda i: (i % nrope, 0))
    const = lambda a: pl.BlockSpec(a.shape, lambda i: (0,) * a.ndim)
    sds = lambda w: jax.ShapeDtypeStruct((n, w), F32)
    return pl.pallas_call(
        _inproj_kernel, grid=(n // tm,),
        in_specs=[row(D_MODEL), const(gamma), const(w_all), rope_row, rope_row],
        out_specs=[row(RKV_W), row(LORA_W), row(B_WIDTH), row(KV_WIDTH), row(KV_WIDTH)],
        out_shape=[sds(RKV_W), sds(LORA_W), sds(B_WIDTH), sds(KV_WIDTH), sds(KV_WIDTH)],
        compiler_params=_params(("parallel",)), name="in_proj",
    )(x, gamma, w_all, cos, sin)


def _rwkv_prep(xs_rkv, xs_lo, w0, wd, a0, wa, wg, k_k, k_a):
    r = xs_rkv[:, 0:A_WIDTH]
    k = xs_rkv[:, A_WIDTH:2 * A_WIDTH]
    v = xs_rkv[:, 2 * A_WIDTH:3 * A_WIDTH]
    z = w0 + _bdot(jnp.tanh(xs_lo[:, 0:LORA_A_OFF]), wd)
    logw = -EXP_NEG_HALF * jax.nn.sigmoid(z)
    a = jax.nn.sigmoid(a0 + _bdot(xs_lo[:, LORA_A_OFF:LORA_G_OFF], wa))
    g = _bdot(jax.nn.sigmoid(xs_lo[:, LORA_G_OFF:LORA_W]), wg)
    kk = k * k_k
    parts = []
    for j in range(A_WIDTH // LANES):
        t = kk[:, j * LANES:(j + 1) * LANES]
        nrm = jnp.sqrt(_head_sum(t * t, 1))
        parts.append(t / jnp.maximum(nrm, 1e-12))
    kk = jnp.concatenate(parts, axis=1)
    k_mod = k * (1.0 + (a - 1.0) * k_a)
    return r, logw, k_mod, v, -kk, kk * a, g


def _rwkv_post(y, r, k_mod, v, g, r_k, lnx_g, lnx_b):
    parts = []
    for j in range(A_WIDTH // LANES):
        ln = slice(j * LANES, (j + 1) * LANES)
        yj = y[:, ln]
        mean = _head_sum(yj, 2) * (1.0 / HEAD_DIM)
        d = yj - mean
        var = _head_sum(d * d, 1) * (1.0 / HEAD_DIM)
        yn = d * lax.rsqrt(var + LNX_EPS) * lnx_g[:, ln] + lnx_b[:, ln]
        bonus = _head_sum(r[:, ln] * k_mod[:, ln] * r_k[:, ln], 2) * v[:, ln]
        parts.append((yn + bonus) * g[:, ln])
    return jnp.concatenate(parts, axis=1)


def _rwkv_chunk_kernel(rkv_ref, lo_ref, mu_rkv_ref, mu_lo_ref, w0_ref, wd_ref, a0_ref, wa_ref, wg_ref,
                       kk_ref, ka_ref, rk_ref, lng_ref, lnb_ref,
                       ya_ref, s_out_ref,
                       s_scr, carry_rkv, carry_lo, r_s, lw_s, km_s, v_s, an_s, bn_s, g_s, y_s, cum_s, *, nb, tc, nt):
    t = pl.program_id(0)

    @pl.when(t == 0)
    def _():
        s_scr[...] = jnp.zeros_like(s_scr)
        carry_rkv[...] = jnp.zeros_like(carry_rkv)
        carry_lo[...] = jnp.zeros_like(carry_lo)

    row0 = _iota((tc, 1), 0) == 0
    ti = _iota((tc, tc), 0)
    tj = _iota((tc, tc), 1)
    tri = ((ti // CHUNK == tj // CHUNK) & (ti >= tj)).astype(BF16)
    for b in range(nb):
        rkv = rkv_ref[b]
        lo = lo_ref[b]
        last = slice(8 * b + 7, 8 * b + 8)
        prev_rkv = jnp.where(row0, carry_rkv[last, :], pltpu.roll(rkv, 1, 0))
        prev_lo = jnp.where(row0, carry_lo[last, :], pltpu.roll(lo, 1, 0))
        carry_rkv[8 * b:8 * b + 8, :] = rkv[tc - 8:tc, :]
        carry_lo[8 * b:8 * b + 8, :] = lo[tc - 8:tc, :]
        xs_rkv = rkv + mu_rkv_ref[...] * (prev_rkv - rkv)
        xs_lo = lo + mu_lo_ref[...] * (prev_lo - lo)
        r, logw, k_mod, v, a_neg, b_pos, g = _rwkv_prep(
            xs_rkv, xs_lo, w0_ref[...], wd_ref[...], a0_ref[...], wa_ref[...], wg_ref[...], kk_ref[...], ka_ref[...])
        blk = slice(b * tc, (b + 1) * tc)
        r_s[blk, :] = r
        lw_s[blk, :] = logw
        km_s[blk, :] = k_mod
        v_s[blk, :] = v
        an_s[blk, :] = a_neg
        bn_s[blk, :] = b_pos
        g_s[blk, :] = g
        cum_s[blk, :] = sum(_dot(tri, piece) for piece in _split_bf16(logw, 3))

    ri = _iota((LANES, LANES), 0)
    rj = _iota((LANES, LANES), 1)
    same = (ri // CHUNK) == (rj // CHUNK)
    strict = same & (ri > rj)
    incl = same & (ri >= rj)
    eye = (ri == rj).astype(F32)
    lo_lanes = _iota((1, LANES), 1) < HEAD_DIM

    def stack(x):
        return jnp.concatenate([jnp.where(lo_lanes, x, 0.0), jnp.where(lo_lanes, 0.0, x)], axis=0)

    def chunk(c, carry):
        pairs = range(nb * HEAD_PAIRS)
        lanes = [slice((p % HEAD_PAIRS) * LANES, (p % HEAD_PAIRS + 1) * LANES) for p in pairs]
        rowss = [pl.ds(pl.multiple_of((p // HEAD_PAIRS) * tc + c * CHUNK, CHUNK), CHUNK) for p in pairs]
        ar, bk_end, v2, w_end, gram = [], [], [], [], []
        for p in pairs:
            ln = lanes[p]
            rows = rowss[p]
            lw = lw_s[rows, ln]
            cum = cum_s[rows, ln]
            w_inv = jnp.exp(-cum)
            w_end.append(jnp.exp(cum[CHUNK - 1:CHUNK, :]))
            a2 = stack(an_s[rows, ln] * jnp.exp(cum - lw))
            r2 = stack(r_s[rows, ln] * jnp.exp(cum))
            b2 = stack(bn_s[rows, ln] * w_inv)
            k2 = stack(km_s[rows, ln] * w_inv)
            v2.append(stack(v_s[rows, ln]).astype(BF16))
            ar.append(jnp.concatenate([a2, r2], axis=0).astype(BF16))
            bk = jnp.concatenate([b2, k2], axis=0)
            bk_end.append((bk * w_end[p]).astype(BF16))
            gram.append(_bdot_nt(ar[p], bk))
        a_ab = [jnp.where(strict, gram[p][0:LANES, 0:LANES], 0.0) for p in pairs]
        a_ak = [jnp.where(strict, gram[p][0:LANES, LANES:], 0.0) for p in pairs]
        incl2 = jnp.concatenate([incl, incl], axis=1)
        q_bk = [jnp.where(incl2, gram[p][LANES:, :], 0.0) for p in pairs]
        npow = [_bdot(a_ab[p], a_ab[p]) for p in pairs]
        tinv = [eye + a_ab[p] for p in pairs]
        for _ in range(4):
            both = [_bdot(jnp.concatenate([npow[p], tinv[p]], axis=0), npow[p]) for p in pairs]
            npow = [both[p][0:LANES, :] for p in pairs]
            tinv = [tinv[p] + both[p][LANES:, :] for p in pairs]
        tinv = [tinv[p] + _bdot(tinv[p], npow[p]) for p in pairs]
        akv = [_bdot(a_ak[p], v2[p]) for p in pairs]
        s2 = [s_scr[p] for p in pairs]
        ars = [_bdot_nt(ar[p], s2[p]) for p in pairs]
        u2 = [_bdot(tinv[p], ars[p][0:LANES, :] + akv[p]) for p in pairs]
        uv = [jnp.concatenate([u2[p].astype(BF16), v2[p]], axis=0) for p in pairs]
        y2 = [ars[p][LANES:, :] + _bdot(q_bk[p], uv[p]) for p in pairs]
        uv_t = [jnp.concatenate([u2[p].T, v2[p].astype(F32).T], axis=1) for p in pairs]
        s_new = [s2[p] * w_end[p] + _bdot(uv_t[p], bk_end[p]) for p in pairs]
        for p in pairs:
            y_s[rowss[p], lanes[p]] = y2[p][0:CHUNK, :] + y2[p][CHUNK:, :]
            s_scr[p] = s_new[p]
        return carry

    lax.fori_loop(0, tc // CHUNK, chunk, 0)

    for b in range(nb):
        blk = slice(b * tc, (b + 1) * tc)
        ya_ref[b] = _rwkv_post(y_s[blk, :], r_s[blk, :], km_s[blk, :], v_s[blk, :], g_s[blk, :],
                               rk_ref[...], lng_ref[...], lnb_ref[...])

    @pl.when(t == nt - 1)
    def _():
        for p in range(nb * HEAD_PAIRS):
            s2 = s_scr[p]
            b, h = p // HEAD_PAIRS, 2 * (p % HEAD_PAIRS)
            s_out_ref[b, h] = s2[0:HEAD_DIM, 0:HEAD_DIM]
            s_out_ref[b, h + 1] = s2[HEAD_DIM:, HEAD_DIM:]


def _rwkv_chunk(rkv, lo, wl, nb, tc):
    n = rkv.shape[0]
    nt = n // nb // tc
    row = lambda w: pl.BlockSpec((nb, tc, w), lambda t: (0, t, 0))
    const = lambda a: pl.BlockSpec(a.shape, lambda t: (0,) * a.ndim)
    consts = [wl[k] for k in ("mu_rkv", "mu_lo", "w0", "wd", "a0", "wa", "wg", "k_k", "k_a", "r_k", "lnx_g", "lnx_b")]
    wide = lambda: pltpu.VMEM((nb * tc, A_WIDTH), F32)
    ya, state = pl.pallas_call(
        functools.partial(_rwkv_chunk_kernel, nb=nb, tc=tc, nt=nt), grid=(nt,),
        in_specs=[row(RKV_W), row(LORA_W)] + [const(a) for a in consts],
        out_specs=[row(A_WIDTH), pl.BlockSpec((nb, N_A_HEADS, HEAD_DIM, HEAD_DIM), lambda t: (0, 0, 0, 0))],
        out_shape=[jax.ShapeDtypeStruct((nb, n // nb, A_WIDTH), F32),
                   jax.ShapeDtypeStruct((nb, N_A_HEADS, HEAD_DIM, HEAD_DIM), F32)],
        scratch_shapes=[pltpu.VMEM((nb * HEAD_PAIRS, LANES, LANES), F32), pltpu.VMEM((8 * nb, RKV_W), F32),
                        pltpu.VMEM((8 * nb, LORA_W), F32)] + [wide() for _ in range(9)],
        compiler_params=_params(("arbitrary",)), name="rwkv_chunk",
    )(rkv.reshape(nb, n // nb, RKV_W), lo.reshape(nb, n // nb, LORA_W), *consts)
    return ya.reshape(n, A_WIDTH), state


def _rwkv_step_kernel(rkv_ref, lo_ref, prkv_ref, plo_ref, s_ref, mu_rkv_ref, mu_lo_ref, w0_ref, wd_ref, a0_ref,
                      wa_ref, wg_ref, kk_ref, ka_ref, rk_ref, lng_ref, lnb_ref,
                      ya_ref, s_out_ref,
                      r_s, w_s, km_s, v_s, an_s, bn_s, y_s, y_odd_s, *, bb):
    rkv = rkv_ref[...]
    lo = lo_ref[...]
    xs_rkv = rkv + mu_rkv_ref[...] * (prkv_ref[...] - rkv)
    xs_lo = lo + mu_lo_ref[...] * (plo_ref[...] - lo)
    r, logw, k_mod, v, a_neg, b_pos, g = _rwkv_prep(
        xs_rkv, xs_lo, w0_ref[...], wd_ref[...], a0_ref[...], wa_ref[...], wg_ref[...], kk_ref[...], ka_ref[...])
    for ref, val in ((r_s, r), (w_s, jnp.exp(logw)), (km_s, k_mod), (v_s, v), (an_s, a_neg), (bn_s, b_pos)):
        ref[:, 0:A_WIDTH] = val
        ref[:, A_WIDTH:] = pltpu.roll(val, HEAD_DIM, 1)
    y_s[...] = jnp.zeros_like(y_s)
    y_odd_s[...] = jnp.zeros_like(y_odd_s)
    eye = _iota((HEAD_DIM, HEAD_DIM), 0) == _iota((HEAD_DIM, HEAD_DIM), 1)

    def vec(ref, bi, h):
        off = h * HEAD_DIM if h % 2 == 0 else A_WIDTH + ((h + 1) * HEAD_DIM) % A_WIDTH
        return ref[bi:bi + 1, off:off + HEAD_DIM]

    heads = range(N_A_HEADS)
    for bi in range(bb):
        s0 = [s_ref[bi, h] for h in heads]
        sa = [jnp.sum(s0[h] * vec(an_s, bi, h), axis=-1, keepdims=True) for h in heads]
        v_col = [jnp.sum(jnp.where(eye, vec(v_s, bi, h), 0.0), axis=-1, keepdims=True) for h in heads]
        s1 = [s0[h] * vec(w_s, bi, h) + sa[h] * vec(bn_s, bi, h) + v_col[h] * vec(km_s, bi, h) for h in heads]
        for h in heads:
            s_out_ref[bi, h] = s1[h]
        y_col = [jnp.sum(s1[h] * vec(r_s, bi, h), axis=-1, keepdims=True) for h in heads]
        for h in heads:
            y_row = jnp.sum(jnp.where(eye, y_col[h], 0.0), axis=0, keepdims=True)
            half = y_s if h % 2 == 0 else y_odd_s
            half[bi:bi + 1, (h // 2) * LANES:(h // 2) * LANES + HEAD_DIM] = y_row
    lo_lanes = _iota((1, A_WIDTH), 1) % LANES < HEAD_DIM
    y = jnp.where(lo_lanes, y_s[...], pltpu.roll(y_odd_s[...], HEAD_DIM, 1))
    ya_ref[...] = _rwkv_post(y, r, k_mod, v, g, rk_ref[...], lng_ref[...], lnb_ref[...])


def _rwkv_step(rkv, lo, prev_rkv, prev_lo, state, wl, bb):
    n = rkv.shape[0]
    row = lambda w: pl.BlockSpec((bb, w), lambda i: (i, 0))
    st = pl.BlockSpec((bb, N_A_HEADS, HEAD_DIM, HEAD_DIM), lambda i: (i, 0, 0, 0))
    const = lambda a: pl.BlockSpec(a.shape, lambda i: (0,) * a.ndim)
    consts = [wl[k] for k in ("mu_rkv", "mu_lo", "w0", "wd", "a0", "wa", "wg", "k_k", "k_a", "r_k", "lnx_g", "lnx_b")]
    return pl.pallas_call(
        functools.partial(_rwkv_step_kernel, bb=bb), grid=(n // bb,),
        in_specs=[row(RKV_W), row(LORA_W), row(RKV_W), row(LORA_W), st] + [const(a) for a in consts],
        out_specs=[row(A_WIDTH), st],
        out_shape=[jax.ShapeDtypeStruct((n, A_WIDTH), F32), jax.ShapeDtypeStruct(state.shape, F32)],
        scratch_shapes=[pltpu.VMEM((bb, 2 * A_WIDTH), F32) for _ in range(6)]
        + [pltpu.VMEM((bb, A_WIDTH), F32) for _ in range(2)],
        compiler_params=_params(("parallel",)), name="rwkv_step",
    )(rkv, lo, prev_rkv, prev_lo, state, *consts)


def _dup_kv(x):
    lo_lanes = _iota((1, LANES), 1) < HEAD_DIM
    xr = pltpu.roll(x, HEAD_DIM, 1)
    return [jnp.where(lo_lanes, x, xr), jnp.where(lo_lanes, xr, x)]


def _sink_attention(q, kdup, vdup, sink_ref, mask):
    lo_lanes = _iota((1, LANES), 1) < HEAD_DIM
    outs = []
    for j in range(B_WIDTH // LANES):
        g = (2 * j) // GQA_GROUP
        q128 = q[:, j * LANES:(j + 1) * LANES]
        halves = []
        for e in range(2):
            sink = sink_ref[2 * j + e]
            qm = jnp.where(lo_lanes, q128, 0.0) if e == 0 else jnp.where(lo_lanes, 0.0, q128)
            s = _dot_nt(qm, kdup[g]) * ATTN_SCALE
            if mask is not None:
                s = jnp.where(mask, s, -jnp.inf)
            m = jnp.maximum(jnp.max(s, axis=-1, keepdims=True), sink)
            p = jnp.exp(s - m)
            den = jnp.sum(p, axis=-1, keepdims=True) + jnp.exp(sink - m)
            halves.append(_dot(p, vdup[g]) / den)
        outs.append(jnp.where(lo_lanes, halves[0], halves[1]))
    return jnp.concatenate(outs, axis=1)


def _swa_banded_kernel(sink_ref, q_ref, kc_ref, kp_ref, vc_ref, vp_ref, o_ref):
    n = pl.program_id(1)
    kcat = jnp.concatenate([kp_ref[...], kc_ref[...]], axis=0)
    vcat = jnp.concatenate([vp_ref[...], vc_ref[...]], axis=0)
    qi = _iota((WINDOW, 2 * WINDOW), 0)
    kj = _iota((WINDOW, 2 * WINDOW), 1)
    diff = qi - kj + WINDOW
    mask = (diff >= 0) & (diff < WINDOW) & ((kj >= WINDOW) | (n > 0))
    o_ref[...] = _sink_attention(q_ref[...], _dup_kv(kcat), _dup_kv(vcat), sink_ref, mask)


def _swa_banded(q, k, v, sinks, nb):
    n = q.shape[0]
    nblk = n // nb // WINDOW
    cur = lambda w: pl.BlockSpec((WINDOW, w), lambda b, i: (b * nblk + i, 0))
    prev = lambda w: pl.BlockSpec((WINDOW, w), lambda b, i: (b * nblk + jnp.maximum(i - 1, 0), 0))
    return pl.pallas_call(
        _swa_banded_kernel, grid=(nb, nblk),
        in_specs=[pl.BlockSpec(memory_space=pltpu.SMEM), cur(B_WIDTH), cur(KV_WIDTH), prev(KV_WIDTH),
                  cur(KV_WIDTH), prev(KV_WIDTH)],
        out_specs=cur(B_WIDTH),
        out_shape=jax.ShapeDtypeStruct((n, B_WIDTH), F32),
        compiler_params=_params(("parallel", "parallel")), name="swa_banded",
    )(sinks, q, k, k, v, v)


def _swa_step_kernel(sink_ref, q_ref, k_ref, v_ref, kb_ref, vb_ref, o_ref, ko_ref, vo_ref, *, bb):
    last = _iota((WINDOW, 1), 0) == WINDOW - 1
    head = _iota((N_Q_HEADS, 1), 0)
    lane = _iota((1, LANES), 1)
    own_half = (lane < HEAD_DIM) == (head % 2 == 0)
    own_keys = (_iota((1, 2 * WINDOW), 1) // WINDOW) == (head // GQA_GROUP)
    sink = jnp.zeros((N_Q_HEADS, 1), F32)
    for h in range(N_Q_HEADS):
        sink = jnp.where(head == h, sink_ref[h], sink)
    lo_lanes = lane < HEAD_DIM

    rows = range(bb)
    kcat, vcat, q8 = [], [], []
    for bi in rows:
        kc = jnp.where(last, k_ref[bi:bi + 1, :], pltpu.roll(kb_ref[bi], WINDOW - 1, 0))
        vc = jnp.where(last, v_ref[bi:bi + 1, :], pltpu.roll(vb_ref[bi], WINDOW - 1, 0))
        ko_ref[bi] = kc
        vo_ref[bi] = vc
        kcat.append(jnp.concatenate(_dup_kv(kc), axis=0))
        vcat.append(jnp.concatenate(_dup_kv(vc), axis=0))
        q = q_ref[bi:bi + 1, :]
        qsel = q[:, 0:LANES]
        for j in range(1, B_WIDTH // LANES):
            qsel = jnp.where(head // 2 == j, q[:, j * LANES:(j + 1) * LANES], qsel)
        q8.append(jnp.where(own_half, qsel, 0.0))
    s = [jnp.where(own_keys, _dot_nt(q8[bi], kcat[bi]) * ATTN_SCALE, -jnp.inf) for bi in rows]
    m = [jnp.maximum(jnp.max(s[bi], axis=-1, keepdims=True), sink) for bi in rows]
    p = [jnp.exp(s[bi] - m[bi]) for bi in rows]
    den = [jnp.sum(p[bi], axis=-1, keepdims=True) + jnp.exp(sink - m[bi]) for bi in rows]
    o = [_dot(p[bi], vcat[bi]) / den[bi] for bi in rows]
    for bi in rows:
        for j in range(B_WIDTH // LANES):
            o_ref[bi:bi + 1, j * LANES:(j + 1) * LANES] = jnp.where(
                lo_lanes, o[bi][2 * j:2 * j + 1, :], o[bi][2 * j + 1:2 * j + 2, :])


def _swa_step(q, k, v, kbuf, vbuf, sinks, bb):
    n = q.shape[0]
    row = lambda w: pl.BlockSpec((bb, w), lambda i: (i, 0))
    buf = pl.BlockSpec((bb, WINDOW, KV_WIDTH), lambda i: (i, 0, 0))
    return pl.pallas_call(
        functools.partial(_swa_step_kernel, bb=bb), grid=(n // bb,),
        in_specs=[pl.BlockSpec(memory_space=pltpu.SMEM), row(B_WIDTH), row(KV_WIDTH), row(KV_WIDTH), buf, buf],
        out_specs=[row(B_WIDTH), buf, buf],
        out_shape=[jax.ShapeDtypeStruct((n, B_WIDTH), F32), jax.ShapeDtypeStruct(kbuf.shape, F32),
                   jax.ShapeDtypeStruct(vbuf.shape, F32)],
        compiler_params=_params(("parallel",)), name="swa_step",
    )(sinks, q, k, v, kbuf, vbuf)


def _mix_ffn_kernel(x_ref, ya_ref, yb_ref, wo_ref, g_ref, wg_ref, wu_ref, wd_ref, fg_ref, o_ref, *, tf, final):
    x = (x_ref[...] + _dot(ya_ref[...].astype(BF16), wo_ref[0:A_WIDTH, :])
         + _dot(yb_ref[...].astype(BF16), wo_ref[A_WIDTH:, :]))
    h = _rms(x, g_ref[...]).astype(BF16)
    o_ref[...] = x
    for f in range(D_FF // tf):
        cols = slice(f * tf, (f + 1) * tf)
        gate = _dot(h, wg_ref[:, cols])
        up = _dot(h, wu_ref[:, cols])
        act = (gate * jax.nn.sigmoid(gate) * up).astype(BF16)
        o_ref[...] += _dot(act, wd_ref[cols, :])
    if final:
        o_ref[...] = _rms(o_ref[...], fg_ref[...])


def _mix_ffn(x, ya, yb, w_out, gamma, w_gate, w_up, w_down, final_gamma, final, tm, tf):
    n = x.shape[0]
    row = lambda w: pl.BlockSpec((tm, w), lambda i: (i, 0))
    resident = lambda a: pl.BlockSpec(memory_space=pltpu.VMEM)
    return pl.pallas_call(
        functools.partial(_mix_ffn_kernel, tf=tf, final=final), grid=(n // tm,),
        in_specs=[row(D_MODEL), row(A_WIDTH), row(B_WIDTH), resident(w_out), resident(gamma), resident(w_gate),
                  resident(w_up), resident(w_down), resident(final_gamma)],
        out_specs=row(D_MODEL), out_shape=jax.ShapeDtypeStruct((n, D_MODEL), F32),
        compiler_params=_params(("parallel",)), name="mix_ffn",
    )(x, ya, yb, w_out, gamma, w_gate, w_up, w_down, final_gamma)


def _pad_lora_cols(t):
    z = lambda n: jnp.zeros(t.shape[:-1] + (n,), t.dtype)
    return jnp.concatenate([t[..., 0:64], z(64), t[..., 64:128], z(64), t[..., 128:288], z(96)], axis=-1)


def _unpad_lora_cols(t):
    return jnp.concatenate([t[..., 0:64], t[..., LORA_A_OFF:LORA_A_OFF + 64],
                            t[..., LORA_G_OFF:LORA_G_OFF + GATE_LORA]], axis=-1)


def _pad_rows(t, rows):
    return jnp.concatenate([t, jnp.zeros((rows - t.shape[0],) + t.shape[1:], t.dtype)], axis=0)


def _rope_tables(positions):
    half = HEAD_DIM // 2
    inv = ROPE_THETA ** (-jnp.arange(half, dtype=F32) / half)
    ang = positions.astype(F32)[:, None] * inv[None, :]
    cos = jnp.cos(ang)
    sin = jnp.sin(ang)
    return jnp.tile(cos, (1, LANES // half)), jnp.tile(jnp.concatenate([-sin, sin], axis=1), (1, LANES // HEAD_DIM))


def _layer_weights(l, attn_norm, w_in, mu, w0, w_decay_up, a0, w_a_up, w_g_up, k_k, k_a, r_k, lnx_g, lnx_b,
                   sinks, w_out, ffn_norm, w_gate, w_up, w_down):
    row = lambda t: t.reshape(1, -1)
    wi = w_in[l]
    w_all = jnp.concatenate([wi[:, 0:RKV_W], _pad_lora_cols(wi[:, RKV_W:A_PROJ]), wi[:, A_PROJ:]], axis=1)
    return dict(
        attn_norm=row(attn_norm[l]), w_all=w_all.astype(BF16),
        mu_rkv=row(mu[l, 0:RKV_W]), mu_lo=row(_pad_lora_cols(mu[l, RKV_W:])),
        w0=row(w0[l]), wd=_pad_rows(w_decay_up[l], LORA_A_OFF), a0=row(a0[l]),
        wa=_pad_rows(w_a_up[l], LORA_G_OFF - LORA_A_OFF), wg=_pad_rows(w_g_up[l], LORA_W - LORA_G_OFF),
        k_k=row(k_k[l]), k_a=row(k_a[l]), r_k=row(r_k[l]), lnx_g=row(lnx_g[l]), lnx_b=row(lnx_b[l]),
        sinks=sinks[l], w_out=w_out[l].astype(BF16), ffn_norm=row(ffn_norm[l]),
        w_gate=w_gate[l].astype(BF16), w_up=w_up[l].astype(BF16), w_down=w_down[l].astype(BF16))


def _last_pa_row(rkv, lo, nb):
    rkv_last = rkv.reshape(nb, -1, RKV_W)[:, -1]
    lo_last = lo.reshape(nb, -1, LORA_W)[:, -1]
    return jnp.concatenate([rkv_last, _unpad_lora_cols(lo_last)], axis=-1)


def kernel(x_prompt, x_sample, state_rwkv, state_shift, cache_k_win, cache_v_win, attn_norm, w_in, mu, w0, w_decay_up, a0, w_a_up, w_g_up, k_k, k_a, r_k, lnx_g, lnx_b, sinks, w_out, ffn_norm, w_gate, w_up, w_down, final_norm):
    bp, tp, _ = x_prompt.shape
    bs, ts, _ = x_sample.shape
    depth = w_in.shape[0]
    assert ts == 1, "sample kernels handle exactly one new token per sequence"
    tm_p = 512 if tp % 512 == 0 else tp
    tc = 256 if tp % 256 == 0 else tp
    tf = 256
    sample_bb = 8

    cos_p, sin_p = _rope_tables(jnp.arange(tp, dtype=jnp.int32))
    cos_s, sin_s = _rope_tables(jnp.full((bs,), PAST_LEN, dtype=jnp.int32))
    final_g = final_norm.reshape(1, -1)

    xp = x_prompt.reshape(bp * tp, D_MODEL)
    xs = x_sample.reshape(bs * ts, D_MODEL)
    p_S, p_sh, p_k, p_v, s_S, s_sh, s_k, s_v = [], [], [], [], [], [], [], []
    for l in range(depth):
        wl = _layer_weights(l, attn_norm, w_in, mu, w0, w_decay_up, a0, w_a_up, w_g_up, k_k, k_a, r_k, lnx_g,
                            lnx_b, sinks, w_out, ffn_norm, w_gate, w_up, w_down)
        final = l == depth - 1

        rkv, lo, q, k, v = _in_proj(xp, wl["attn_norm"], wl["w_all"], cos_p, sin_p, tm_p)
        ya, S = _rwkv_chunk(rkv, lo, wl, bp, tc)
        yb = _swa_banded(q, k, v, wl["sinks"], bp)
        xp = _mix_ffn(xp, ya, yb, wl["w_out"], wl["ffn_norm"], wl["w_gate"], wl["w_up"], wl["w_down"], final_g,
                      final, tm_p, tf)
        p_S.append(S)
        p_sh.append(_last_pa_row(rkv, lo, bp))
        p_k.append(k.reshape(bp, tp, N_KV_HEADS, HEAD_DIM)[:, tp - WINDOW:])
        p_v.append(v.reshape(bp, tp, N_KV_HEADS, HEAD_DIM)[:, tp - WINDOW:])

        rkv, lo, q, k, v = _in_proj(xs, wl["attn_norm"], wl["w_all"], cos_s, sin_s, bs)
        prev = state_shift[l]
        ya, S = _rwkv_step(rkv, lo, prev[:, 0:RKV_W], _pad_lora_cols(prev[:, RKV_W:]), state_rwkv[l], wl, sample_bb)
        yb, kwin, vwin = _swa_step(q, k, v, cache_k_win[l].reshape(bs, WINDOW, KV_WIDTH),
                                   cache_v_win[l].reshape(bs, WINDOW, KV_WIDTH), wl["sinks"], sample_bb)
        xs = _mix_ffn(xs, ya, yb, wl["w_out"], wl["ffn_norm"], wl["w_gate"], wl["w_up"], wl["w_down"], final_g,
                      final, bs, tf)
        s_S.append(S)
        s_sh.append(_last_pa_row(rkv, lo, bs))
        s_k.append(kwin.reshape(bs, WINDOW, N_KV_HEADS, HEAD_DIM))
        s_v.append(vwin.reshape(bs, WINDOW, N_KV_HEADS, HEAD_DIM))

    return (xp.reshape(bp, tp, D_MODEL), xs.reshape(bs, ts, D_MODEL),
            jnp.stack(p_S), jnp.stack(p_sh), jnp.stack(p_k), jnp.stack(p_v),
            jnp.stack(s_S), jnp.stack(s_sh), jnp.stack(s_k), jnp.stack(s_v))
```

```python
import functools

import jax
import jax.numpy as jnp
from jax import lax
from jax.experimental import pallas as pl
from jax.experimental.pallas import tpu as pltpu

F32 = jnp.float32
BF16 = jnp.bfloat16

D_MODEL = 1024
HEAD_DIM = 64
N_A_HEADS = 8
A_WIDTH = N_A_HEADS * HEAD_DIM
N_Q_HEADS = 8
N_KV_HEADS = 2
GQA_GROUP = N_Q_HEADS // N_KV_HEADS
B_WIDTH = N_Q_HEADS * HEAD_DIM
KV_WIDTH = N_KV_HEADS * HEAD_DIM
DECAY_LORA = 64
AAA_LORA = 64
GATE_LORA = 160
A_PROJ = 3 * A_WIDTH + DECAY_LORA + AAA_LORA + GATE_LORA
WINDOW = 128
PAST_LEN = 8192
D_FF = 2816
ROPE_THETA = 10000.0
NORM_EPS = 1e-5
LNX_EPS = 64e-5
ATTN_SCALE = HEAD_DIM ** -0.5
EXP_NEG_HALF = 0.6065306597126334

LANES = 128
HEAD_PAIRS = N_A_HEADS // 2
CHUNK = 64
RKV_W = 3 * A_WIDTH
LORA_W = 512
LORA_A_OFF = 128
LORA_G_OFF = 256
PROJ_W = RKV_W + LORA_W + B_WIDTH + 2 * KV_WIDTH
VMEM_LIMIT = 48 * 1024 * 1024


def _dot(a, b):
    return jnp.dot(a, b, preferred_element_type=F32)


def _dot_nt(a, b):
    return lax.dot_general(a, b, (((1,), (1,)), ((), ())), preferred_element_type=F32)


def _iota(shape, dim):
    return lax.broadcasted_iota(jnp.int32, shape, dim)


def _rms(x, g):
    return x * lax.rsqrt(jnp.mean(x * x, axis=-1, keepdims=True) + NORM_EPS) * g


def _bdot(a, b):
    return jnp.dot(a.astype(BF16), b.astype(BF16), preferred_element_type=F32)


def _bdot_nt(a, b):
    return lax.dot_general(a.astype(BF16), b.astype(BF16), (((1,), (1,)), ((), ())), preferred_element_type=F32)


def _split_bf16(x, parts):
    out = []
    for _ in range(parts):
        piece = x.astype(BF16)
        out.append(piece)
        x = x - piece.astype(F32)
    return out


def _head_sum(x):
    lo_lanes = _iota((1, LANES), 1) < HEAD_DIM
    lo_sum = jnp.sum(jnp.where(lo_lanes, x, 0.0), axis=-1, keepdims=True)
    hi_sum = jnp.sum(jnp.where(lo_lanes, 0.0, x), axis=-1, keepdims=True)
    return jnp.where(lo_lanes, lo_sum, hi_sum)


def _params(sem):
    return pltpu.CompilerParams(dimension_semantics=sem, vmem_limit_bytes=VMEM_LIMIT)


def _layer_spec(a, l, resident=False):
    zeros = (0,) * (a.ndim - 1)
    mode = dict(pipeline_mode=pl.Buffered(1)) if resident else {}
    return pl.BlockSpec((None,) + a.shape[1:], lambda *_: (l,) + zeros, **mode)


def _inproj_kernel(x_ref, g_ref, w_ref, cos_ref, sin_ref, rkv_ref, lo_ref, q_ref, k_ref, v_ref):
    hb = _rms(x_ref[...], g_ref[...]).astype(BF16)
    rkv_ref[...] = _dot(hb, w_ref[:, 0:RKV_W])
    lo_ref[...] = _dot(hb, w_ref[:, RKV_W:RKV_W + LORA_W])
    qkv = _dot(hb, w_ref[:, RKV_W + LORA_W:PROJ_W])
    cos = cos_ref[...]
    sin = sin_ref[...]
    first_half = (_iota((1, LANES), 1) % HEAD_DIM) < (HEAD_DIM // 2)

    def rope(t):
        rot = jnp.where(first_half, pltpu.roll(t, LANES - HEAD_DIM // 2, 1), pltpu.roll(t, HEAD_DIM // 2, 1))
        return t * cos + rot * sin

    for j in range(B_WIDTH // LANES):
        q_ref[:, j * LANES:(j + 1) * LANES] = rope(qkv[:, j * LANES:(j + 1) * LANES])
    k_ref[...] = rope(qkv[:, B_WIDTH:B_WIDTH + KV_WIDTH])
    v_ref[...] = qkv[:, B_WIDTH + KV_WIDTH:]


def _in_proj(x, gamma, w_all, l, cos, sin, tm):
    n = x.shape[0]
    nrope = cos.shape[0] // tm
    row = lambda w: pl.BlockSpec((tm, w), lambda i: (i, 0))
    rope_row = pl.BlockSpec((tm, LANES), lambda i: (i % nrope, 0))
    sds = lambda w: jax.ShapeDtypeStruct((n, w), F32)
    return pl.pallas_call(
        _inproj_kernel, grid=(n // tm,),
        in_specs=[row(D_MODEL), _layer_spec(gamma, l), _layer_spec(w_all, l, resident=True), rope_row, rope_row],
        out_specs=[row(RKV_W), row(LORA_W), row(B_WIDTH), row(KV_WIDTH), row(KV_WIDTH)],
        out_shape=[sds(RKV_W), sds(LORA_W), sds(B_WIDTH), sds(KV_WIDTH), sds(KV_WIDTH)],
        compiler_params=_params(("parallel",)), name="in_proj",
    )(x, gamma, w_all, cos, sin)


def _rwkv_prep(xs_rkv, xs_lo, w0, wd, a0, wa, wg, k_k, k_a):
    r = xs_rkv[:, 0:A_WIDTH]
    k = xs_rkv[:, A_WIDTH:2 * A_WIDTH]
    v = xs_rkv[:, 2 * A_WIDTH:3 * A_WIDTH]
    z = w0 + _bdot(jnp.tanh(xs_lo[:, 0:LORA_A_OFF]), wd)
    logw = -EXP_NEG_HALF * jax.nn.sigmoid(z)
    a = jax.nn.sigmoid(a0 + _bdot(xs_lo[:, LORA_A_OFF:LORA_G_OFF], wa))
    g = _bdot(jax.nn.sigmoid(xs_lo[:, LORA_G_OFF:LORA_W]), wg)
    kk = k * k_k
    parts = []
    for j in range(A_WIDTH // LANES):
        t = kk[:, j * LANES:(j + 1) * LANES]
        nrm = jnp.sqrt(_head_sum(t * t))
        parts.append(t / jnp.maximum(nrm, 1e-12))
    kk = jnp.concatenate(parts, axis=1)
    k_mod = k * (1.0 + (a - 1.0) * k_a)
    return r, logw, k_mod, v, -kk, kk * a, g


def _rwkv_post(y, r, k_mod, v, g, r_k, lnx_g, lnx_b):
    parts = []
    for j in range(A_WIDTH // LANES):
        ln = slice(j * LANES, (j + 1) * LANES)
        yj = y[:, ln]
        mean = _head_sum(yj) * (1.0 / HEAD_DIM)
        d = yj - mean
        var = _head_sum(d * d) * (1.0 / HEAD_DIM)
        yn = d * lax.rsqrt(var + LNX_EPS) * lnx_g[:, ln] + lnx_b[:, ln]
        bonus = _head_sum(r[:, ln] * k_mod[:, ln] * r_k[:, ln]) * v[:, ln]
        parts.append((yn + bonus) * g[:, ln])
    return jnp.concatenate(parts, axis=1)


def _rwkv_chunk_kernel(rkv_ref, lo_ref, mu_rkv_ref, mu_lo_ref, w0_ref, wd_ref, a0_ref, wa_ref, wg_ref,
                       kk_ref, ka_ref, rk_ref, lng_ref, lnb_ref,
                       ya_ref, s_out_ref,
                       s_scr, carry_rkv, carry_lo, r_s, lw_s, km_s, v_s, an_s, bn_s, g_s, cum_s, y_s, *, nb, tb, nt):
    t = pl.program_id(0)

    @pl.when(t == 0)
    def _():
        for ref in (s_scr, carry_rkv, carry_lo, r_s, lw_s, km_s, v_s, an_s, bn_s, g_s, cum_s):
            ref[...] = jnp.zeros_like(ref)

    slot = t % 2
    done = 1 - slot

    row0 = _iota((tb, 1), 0) == 0
    ti = _iota((tb, tb), 0)
    tj = _iota((tb, tb), 1)
    tri = ((ti // CHUNK == tj // CHUNK) & (ti >= tj)).astype(BF16)
    for b in range(nb):
        rkv = rkv_ref[b]
        lo = lo_ref[b]
        last = slice(8 * b + 7, 8 * b + 8)
        prev_rkv = jnp.where(row0, carry_rkv[last, :], pltpu.roll(rkv, 1, 0))
        prev_lo = jnp.where(row0, carry_lo[last, :], pltpu.roll(lo, 1, 0))
        carry_rkv[8 * b:8 * b + 8, :] = rkv[tb - 8:tb, :]
        carry_lo[8 * b:8 * b + 8, :] = lo[tb - 8:tb, :]
        xs_rkv = rkv + mu_rkv_ref[...] * (prev_rkv - rkv)
        xs_lo = lo + mu_lo_ref[...] * (prev_lo - lo)
        r, logw, k_mod, v, a_neg, b_pos, g = _rwkv_prep(
            xs_rkv, xs_lo, w0_ref[...], wd_ref[...], a0_ref[...], wa_ref[...], wg_ref[...], kk_ref[...], ka_ref[...])
        blk = slice(b * tb, (b + 1) * tb)
        r_s[slot, blk, :] = r
        lw_s[slot, blk, :] = logw
        km_s[slot, blk, :] = k_mod
        v_s[slot, blk, :] = v
        an_s[slot, blk, :] = a_neg
        bn_s[slot, blk, :] = b_pos
        g_s[slot, blk, :] = g
        cum_s[slot, blk, :] = sum(_dot(tri, piece) for piece in _split_bf16(logw, 3))

    ri = _iota((LANES, LANES), 0)
    rj = _iota((LANES, LANES), 1)
    same = (ri // CHUNK) == (rj // CHUNK)
    strict = same & (ri > rj)
    incl = same & (ri >= rj)
    incl2 = jnp.concatenate([incl, incl], axis=1)
    eye = (ri == rj).astype(F32)
    lo_lanes = _iota((1, LANES), 1) < HEAD_DIM

    def stack(x):
        return jnp.concatenate([jnp.where(lo_lanes, x, 0.0), jnp.where(lo_lanes, 0.0, x)], axis=0)

    pairs = range(nb * HEAD_PAIRS)
    lanes = [slice((p % HEAD_PAIRS) * LANES, (p % HEAD_PAIRS + 1) * LANES) for p in pairs]
    s2 = [s_scr[p] for p in pairs]
    for c in range(tb // CHUNK):
        rowss = [slice((p // HEAD_PAIRS) * tb + c * CHUNK, (p // HEAD_PAIRS) * tb + (c + 1) * CHUNK) for p in pairs]
        ar, bk_end, v2, w_end, gram = [], [], [], [], []
        for p in pairs:
            ln = lanes[p]
            rows = rowss[p]
            lw = lw_s[done, rows, ln]
            cum = cum_s[done, rows, ln]
            w_inv = jnp.exp(-cum)
            w_end.append(jnp.exp(cum[CHUNK - 1:CHUNK, :]))
            a2 = stack(an_s[done, rows, ln] * jnp.exp(cum - lw))
            r2 = stack(r_s[done, rows, ln] * jnp.exp(cum))
            b2 = stack(bn_s[done, rows, ln] * w_inv)
            k2 = stack(km_s[done, rows, ln] * w_inv)
            v2.append(stack(v_s[done, rows, ln]).astype(BF16))
            ar.append(jnp.concatenate([a2, r2], axis=0).astype(BF16))
            bk = jnp.concatenate([b2, k2], axis=0)
            bk_end.append((bk * w_end[p]).astype(BF16))
            gram.append(_bdot_nt(ar[p], bk))
        a_ab = [jnp.where(strict, gram[p][0:LANES, 0:LANES], 0.0) for p in pairs]
        a_ak = [jnp.where(strict, gram[p][0:LANES, LANES:], 0.0) for p in pairs]
        q_bk = [jnp.where(incl2, gram[p][LANES:, :], 0.0) for p in pairs]
        npow = [_bdot(a_ab[p], a_ab[p]) for p in pairs]
        tinv = [eye + a_ab[p] for p in pairs]
        for _ in range(4):
            both = [_bdot(jnp.concatenate([npow[p], tinv[p]], axis=0), npow[p]) for p in pairs]
            npow = [both[p][0:LANES, :] for p in pairs]
            tinv = [tinv[p] + both[p][LANES:, :] for p in pairs]
        tinv = [tinv[p] + _bdot(tinv[p], npow[p]) for p in pairs]
        akv = [_bdot(a_ak[p], v2[p]) for p in pairs]
        ars = [_bdot_nt(ar[p], s2[p]) for p in pairs]
        u2 = [_bdot(tinv[p], ars[p][0:LANES, :] + akv[p]) for p in pairs]
        uv = [jnp.concatenate([u2[p].astype(BF16), v2[p]], axis=0) for p in pairs]
        y2 = [ars[p][LANES:, :] + _bdot(q_bk[p], uv[p]) for p in pairs]
        uv_t = [jnp.concatenate([u2[p].T, v2[p].astype(F32).T], axis=1) for p in pairs]
        s2 = [s2[p] * w_end[p] + _bdot(uv_t[p], bk_end[p]) for p in pairs]
        for p in pairs:
            y_s[rowss[p], lanes[p]] = y2[p][0:CHUNK, :] + y2[p][CHUNK:, :]
    for p in pairs:
        s_scr[p] = s2[p]

    for b in range(nb):
        blk = slice(b * tb, (b + 1) * tb)
        ya_ref[b] = _rwkv_post(y_s[blk, :], r_s[done, blk, :], km_s[done, blk, :], v_s[done, blk, :],
                               g_s[done, blk, :], rk_ref[...], lng_ref[...], lnb_ref[...])

    @pl.when(t == nt)
    def _():
        for p in pairs:
            b, h = p // HEAD_PAIRS, 2 * (p % HEAD_PAIRS)
            s_out_ref[b, h] = s2[p][0:HEAD_DIM, 0:HEAD_DIM]
            s_out_ref[b, h + 1] = s2[p][HEAD_DIM:, HEAD_DIM:]


_RWKV_PARAMS = ("mu_rkv", "mu_lo", "w0", "wd", "a0", "wa", "wg", "k_k", "k_a", "r_k", "lnx_g", "lnx_b")


def _rwkv_chunk(rkv, lo, wts, l, nb, tb):
    n = rkv.shape[0]
    nt = n // nb // tb
    blk_in = lambda w: pl.BlockSpec((nb, tb, w), lambda t: (0, jnp.minimum(t, nt - 1), 0))
    blk_out = pl.BlockSpec((nb, tb, A_WIDTH), lambda t: (0, jnp.maximum(t - 1, 0), 0))
    consts = [wts[k] for k in _RWKV_PARAMS]
    wide = lambda: pltpu.VMEM((2, nb * tb, A_WIDTH), F32)
    ya, state = pl.pallas_call(
        functools.partial(_rwkv_chunk_kernel, nb=nb, tb=tb, nt=nt), grid=(nt + 1,),
        in_specs=[blk_in(RKV_W), blk_in(LORA_W)] + [_layer_spec(a, l) for a in consts],
        out_specs=[blk_out, pl.BlockSpec((nb, N_A_HEADS, HEAD_DIM, HEAD_DIM), lambda t: (0, 0, 0, 0))],
        out_shape=[jax.ShapeDtypeStruct((nb, n // nb, A_WIDTH), F32),
                   jax.ShapeDtypeStruct((nb, N_A_HEADS, HEAD_DIM, HEAD_DIM), F32)],
        scratch_shapes=[pltpu.VMEM((nb * HEAD_PAIRS, LANES, LANES), F32), pltpu.VMEM((8 * nb, RKV_W), F32),
                        pltpu.VMEM((8 * nb, LORA_W), F32)] + [wide() for _ in range(8)]
        + [pltpu.VMEM((nb * tb, A_WIDTH), F32)],
        compiler_params=_params(("arbitrary",)), name="rwkv_chunk",
    )(rkv.reshape(nb, n // nb, RKV_W), lo.reshape(nb, n // nb, LORA_W), *consts)
    return ya.reshape(n, A_WIDTH), state


def _rwkv_step_kernel(rkv_ref, lo_ref, prkv_ref, plo_ref, s_ref, mu_rkv_ref, mu_lo_ref, w0_ref, wd_ref, a0_ref,
                      wa_ref, wg_ref, kk_ref, ka_ref, rk_ref, lng_ref, lnb_ref,
                      ya_ref, s_out_ref,
                      r_s, w_s, km_s, v_s, an_s, bn_s, y_s, y_odd_s, *, bb):
    rkv = rkv_ref[...]
    lo = lo_ref[...]
    xs_rkv = rkv + mu_rkv_ref[...] * (prkv_ref[...] - rkv)
    xs_lo = lo + mu_lo_ref[...] * (plo_ref[...] - lo)
    r, logw, k_mod, v, a_neg, b_pos, g = _rwkv_prep(
        xs_rkv, xs_lo, w0_ref[...], wd_ref[...], a0_ref[...], wa_ref[...], wg_ref[...], kk_ref[...], ka_ref[...])
    for ref, val in ((r_s, r), (w_s, jnp.exp(logw)), (km_s, k_mod), (v_s, v), (an_s, a_neg), (bn_s, b_pos)):
        ref[:, 0:A_WIDTH] = val
        ref[:, A_WIDTH:] = pltpu.roll(val, HEAD_DIM, 1)
    y_s[...] = jnp.zeros_like(y_s)
    y_odd_s[...] = jnp.zeros_like(y_odd_s)
    eye = _iota((HEAD_DIM, HEAD_DIM), 0) == _iota((HEAD_DIM, HEAD_DIM), 1)

    def vec(ref, bi, h):
        off = h * HEAD_DIM if h % 2 == 0 else A_WIDTH + ((h + 1) * HEAD_DIM) % A_WIDTH
        return ref[bi:bi + 1, off:off + HEAD_DIM]

    heads = range(N_A_HEADS)
    for bi in range(bb):
        s0 = [s_ref[bi, h] for h in heads]
        sa = [jnp.sum(s0[h] * vec(an_s, bi, h), axis=-1, keepdims=True) for h in heads]
        v_col = [jnp.sum(jnp.where(eye, vec(v_s, bi, h), 0.0), axis=-1, keepdims=True) for h in heads]
        s1 = [s0[h] * vec(w_s, bi, h) + sa[h] * vec(bn_s, bi, h) + v_col[h] * vec(km_s, bi, h) for h in heads]
        for h in heads:
            s_out_ref[bi, h] = s1[h]
        y_col = [jnp.sum(s1[h] * vec(r_s, bi, h), axis=-1, keepdims=True) for h in heads]
        for h in heads:
            y_row = jnp.sum(jnp.where(eye, y_col[h], 0.0), axis=0, keepdims=True)
            half = y_s if h % 2 == 0 else y_odd_s
            half[bi:bi + 1, (h // 2) * LANES:(h // 2) * LANES + HEAD_DIM] = y_row
    lo_lanes = _iota((1, A_WIDTH), 1) % LANES < HEAD_DIM
    y = jnp.where(lo_lanes, y_s[...], pltpu.roll(y_odd_s[...], HEAD_DIM, 1))
    ya_ref[...] = _rwkv_post(y, r, k_mod, v, g, rk_ref[...], lng_ref[...], lnb_ref[...])


def _rwkv_step(rkv, lo, prev_rkv, prev_lo, state, wts, l, bb):
    n = rkv.shape[0]
    row = lambda w: pl.BlockSpec((bb, w), lambda i: (i, 0))
    prev = lambda w: pl.BlockSpec((None, bb, w), lambda i: (l, i, 0))
    st = pl.BlockSpec((None, bb, N_A_HEADS, HEAD_DIM, HEAD_DIM), lambda i: (l, i, 0, 0, 0))
    consts = [wts[k] for k in _RWKV_PARAMS]
    return pl.pallas_call(
        functools.partial(_rwkv_step_kernel, bb=bb), grid=(n // bb,),
        in_specs=[row(RKV_W), row(LORA_W), prev(RKV_W), prev(LORA_W), st] + [_layer_spec(a, l) for a in consts],
        out_specs=[row(A_WIDTH), st],
        out_shape=[jax.ShapeDtypeStruct((n, A_WIDTH), F32), jax.ShapeDtypeStruct(state.shape, F32)],
        input_output_aliases={4: 1},
        scratch_shapes=[pltpu.VMEM((bb, 2 * A_WIDTH), F32) for _ in range(6)]
        + [pltpu.VMEM((bb, A_WIDTH), F32) for _ in range(2)],
        compiler_params=_params(("parallel",)), name="rwkv_step",
    )(rkv, lo, prev_rkv, prev_lo, state, *consts)


def _dup_kv(x):
    lo_lanes = _iota((1, LANES), 1) < HEAD_DIM
    xr = pltpu.roll(x, HEAD_DIM, 1)
    return [jnp.where(lo_lanes, x, xr), jnp.where(lo_lanes, xr, x)]


def _sink_attention(q, kdup, vdup, sink_ref, l, mask):
    rows = q.shape[0]
    lo_lanes = _iota((1, LANES), 1) < HEAD_DIM
    head_of_row = _iota((GQA_GROUP * rows, 1), 0) // rows
    mask4 = jnp.concatenate([mask] * GQA_GROUP, axis=0)
    groups = range(N_KV_HEADS)
    qs, sink = [], []
    for g in groups:
        parts = []
        sink_g = jnp.zeros((GQA_GROUP * rows, 1), F32)
        for r in range(GQA_GROUP):
            h = g * GQA_GROUP + r
            q128 = q[:, (h // 2) * LANES:(h // 2 + 1) * LANES]
            parts.append(jnp.where(lo_lanes, q128, 0.0) if h % 2 == 0 else jnp.where(lo_lanes, 0.0, q128))
            sink_g = jnp.where(head_of_row == r, sink_ref[l, h], sink_g)
        qs.append(jnp.concatenate(parts, axis=0))
        sink.append(sink_g)
    s = [jnp.where(mask4, _bdot_nt(qs[g], kdup[g]) * ATTN_SCALE, -jnp.inf) for g in groups]
    m = [jnp.maximum(jnp.max(s[g], axis=-1, keepdims=True), sink[g]) for g in groups]
    p = [jnp.exp(s[g] - m[g]) for g in groups]
    den = [jnp.sum(p[g], axis=-1, keepdims=True) + jnp.exp(sink[g] - m[g]) for g in groups]
    o = [_bdot(p[g], vdup[g]) / den[g] for g in groups]
    outs = []
    for g in groups:
        for jj in range(GQA_GROUP // 2):
            even = o[g][(2 * jj) * rows:(2 * jj + 1) * rows, :]
            odd = o[g][(2 * jj + 1) * rows:(2 * jj + 2) * rows, :]
            outs.append(jnp.where(lo_lanes, even, odd))
    return jnp.concatenate(outs, axis=1)


def _swa_banded_kernel(sink_ref, q_ref, kc_ref, kp_ref, vc_ref, vp_ref, o_ref, *, l):
    n = pl.program_id(1)
    kcat = jnp.concatenate([kp_ref[...], kc_ref[...]], axis=0)
    vcat = jnp.concatenate([vp_ref[...], vc_ref[...]], axis=0)
    qi = _iota((WINDOW, 2 * WINDOW), 0)
    kj = _iota((WINDOW, 2 * WINDOW), 1)
    diff = qi - kj + WINDOW
    mask = (diff >= 0) & (diff < WINDOW) & ((kj >= WINDOW) | (n > 0))
    o_ref[...] = _sink_attention(q_ref[...], _dup_kv(kcat), _dup_kv(vcat), sink_ref, l, mask)


def _swa_banded(q, k, v, sinks, l, nb):
    n = q.shape[0]
    nblk = n // nb // WINDOW
    cur = lambda w: pl.BlockSpec((WINDOW, w), lambda b, i: (b * nblk + i, 0))
    prev = lambda w: pl.BlockSpec((WINDOW, w), lambda b, i: (b * nblk + jnp.maximum(i - 1, 0), 0))
    return pl.pallas_call(
        functools.partial(_swa_banded_kernel, l=l), grid=(nb, nblk),
        in_specs=[pl.BlockSpec(memory_space=pltpu.SMEM), cur(B_WIDTH), cur(KV_WIDTH), prev(KV_WIDTH),
                  cur(KV_WIDTH), prev(KV_WIDTH)],
        out_specs=cur(B_WIDTH),
        out_shape=jax.ShapeDtypeStruct((n, B_WIDTH), F32),
        compiler_params=_params(("parallel", "parallel")), name="swa_banded",
    )(sinks, q, k, k, v, v)


def _swa_step_kernel(sink_ref, q_ref, k_ref, v_ref, kb_ref, vb_ref, o_ref, ko_ref, vo_ref, *, l, bb):
    last = _iota((WINDOW, 1), 0) == WINDOW - 1
    head = _iota((N_Q_HEADS, 1), 0)
    lane = _iota((1, LANES), 1)
    own_half = (lane < HEAD_DIM) == (head % 2 == 0)
    own_keys = (_iota((1, 2 * WINDOW), 1) // WINDOW) == (head // GQA_GROUP)
    sink = jnp.zeros((N_Q_HEADS, 1), F32)
    for h in range(N_Q_HEADS):
        sink = jnp.where(head == h, sink_ref[l, h], sink)
    lo_lanes = lane < HEAD_DIM

    rows = range(bb)
    kcat, vcat, q8 = [], [], []
    for bi in rows:
        kc = jnp.where(last, k_ref[bi:bi + 1, :], pltpu.roll(kb_ref[bi], WINDOW - 1, 0))
        vc = jnp.where(last, v_ref[bi:bi + 1, :], pltpu.roll(vb_ref[bi], WINDOW - 1, 0))
        ko_ref[bi] = kc
        vo_ref[bi] = vc
        kcat.append(jnp.concatenate(_dup_kv(kc), axis=0))
        vcat.append(jnp.concatenate(_dup_kv(vc), axis=0))
        q = q_ref[bi:bi + 1, :]
        qsel = q[:, 0:LANES]
        for j in range(1, B_WIDTH // LANES):
            qsel = jnp.where(head // 2 == j, q[:, j * LANES:(j + 1) * LANES], qsel)
        q8.append(jnp.where(own_half, qsel, 0.0))
    s = [jnp.where(own_keys, _dot_nt(q8[bi], kcat[bi]) * ATTN_SCALE, -jnp.inf) for bi in rows]
    m = [jnp.maximum(jnp.max(s[bi], axis=-1, keepdims=True), sink) for bi in rows]
    p = [jnp.exp(s[bi] - m[bi]) for bi in rows]
    den = [jnp.sum(p[bi], axis=-1, keepdims=True) + jnp.exp(sink - m[bi]) for bi in rows]
    o = [_dot(p[bi], vcat[bi]) / den[bi] for bi in rows]
    for bi in rows:
        for j in range(B_WIDTH // LANES):
            o_ref[bi:bi + 1, j * LANES:(j + 1) * LANES] = jnp.where(
                lo_lanes, o[bi][2 * j:2 * j + 1, :], o[bi][2 * j + 1:2 * j + 2, :])


def _swa_step(q, k, v, kbuf, vbuf, sinks, l, bb):
    n = q.shape[0]
    row = lambda w: pl.BlockSpec((bb, w), lambda i: (i, 0))
    buf = pl.BlockSpec((None, bb, WINDOW, KV_WIDTH), lambda i: (l, i, 0, 0))
    return pl.pallas_call(
        functools.partial(_swa_step_kernel, l=l, bb=bb), grid=(n // bb,),
        in_specs=[pl.BlockSpec(memory_space=pltpu.SMEM), row(B_WIDTH), row(KV_WIDTH), row(KV_WIDTH), buf, buf],
        out_specs=[row(B_WIDTH), buf, buf],
        out_shape=[jax.ShapeDtypeStruct((n, B_WIDTH), F32), jax.ShapeDtypeStruct(kbuf.shape, F32),
                   jax.ShapeDtypeStruct(vbuf.shape, F32)],
        input_output_aliases={4: 1, 5: 2},
        compiler_params=_params(("parallel",)), name="swa_step",
    )(sinks, q, k, v, kbuf, vbuf)


def _mix_ffn_kernel(x_ref, ya_ref, yb_ref, wo_ref, g_ref, wg_ref, wu_ref, wd_ref, fg_ref, o_ref, *, tf, final):
    x = (x_ref[...] + _dot(ya_ref[...].astype(BF16), wo_ref[0:A_WIDTH, :])
         + _dot(yb_ref[...].astype(BF16), wo_ref[A_WIDTH:, :]))
    h = _rms(x, g_ref[...]).astype(BF16)
    o_ref[...] = x
    for f in range(D_FF // tf):
        cols = slice(f * tf, (f + 1) * tf)
        gate = _dot(h, wg_ref[:, cols])
        up = _dot(h, wu_ref[:, cols])
        act = (gate * jax.nn.sigmoid(gate) * up).astype(BF16)
        o_ref[...] += _dot(act, wd_ref[cols, :])
    if final:
        o_ref[...] = _rms(o_ref[...], fg_ref[...])


def _mix_ffn(x, ya, yb, wts, l, final_gamma, final, tm, tf):
    n = x.shape[0]
    row = lambda w: pl.BlockSpec((tm, w), lambda i: (i, 0))
    weights = [wts[k] for k in ("w_out", "ffn_norm", "w_gate", "w_up", "w_down")]
    return pl.pallas_call(
        functools.partial(_mix_ffn_kernel, tf=tf, final=final), grid=(n // tm,),
        in_specs=[row(D_MODEL), row(A_WIDTH), row(B_WIDTH)] + [_layer_spec(a, l, resident=True) for a in weights]
        + [pl.BlockSpec(final_gamma.shape, lambda i: (0, 0))],
        out_specs=row(D_MODEL), out_shape=jax.ShapeDtypeStruct((n, D_MODEL), F32),
        compiler_params=_params(("parallel",)), name="mix_ffn",
    )(x, ya, yb, *weights, final_gamma)


def _pad_lora_cols(t):
    z = lambda n: jnp.zeros(t.shape[:-1] + (n,), t.dtype)
    return jnp.concatenate([t[..., 0:64], z(64), t[..., 64:128], z(64), t[..., 128:288], z(96)], axis=-1)


def _unpad_lora_cols(t):
    return jnp.concatenate([t[..., 0:64], t[..., LORA_A_OFF:LORA_A_OFF + 64],
                            t[..., LORA_G_OFF:LORA_G_OFF + GATE_LORA]], axis=-1)


def _pad_rows(t, rows):
    return jnp.concatenate([t, jnp.zeros(t.shape[:-2] + (rows - t.shape[-2], t.shape[-1]), t.dtype)], axis=-2)


def _rope_tables(positions):
    half = HEAD_DIM // 2
    inv = ROPE_THETA ** (-jnp.arange(half, dtype=F32) / half)
    ang = positions.astype(F32)[:, None] * inv[None, :]
    cos = jnp.cos(ang)
    sin = jnp.sin(ang)
    return jnp.tile(cos, (1, LANES // half)), jnp.tile(jnp.concatenate([-sin, sin], axis=1), (1, LANES // HEAD_DIM))


def _prep_weights(attn_norm, w_in, mu, w0, w_decay_up, a0, w_a_up, w_g_up, k_k, k_a, r_k, lnx_g, lnx_b, sinks,
                  w_out, ffn_norm, w_gate, w_up, w_down):
    depth = w_in.shape[0]
    vec = lambda t: t.reshape(depth, 1, -1)
    w_all = jnp.concatenate([w_in[..., 0:RKV_W], _pad_lora_cols(w_in[..., RKV_W:A_PROJ]), w_in[..., A_PROJ:]], axis=-1)
    return dict(
        attn_norm=vec(attn_norm), w_all=w_all.astype(BF16),
        mu_rkv=vec(mu[:, 0:RKV_W]), mu_lo=vec(_pad_lora_cols(mu[:, RKV_W:])),
        w0=vec(w0), wd=_pad_rows(w_decay_up, LORA_A_OFF), a0=vec(a0),
        wa=_pad_rows(w_a_up, LORA_G_OFF - LORA_A_OFF), wg=_pad_rows(w_g_up, LORA_W - LORA_G_OFF),
        k_k=vec(k_k), k_a=vec(k_a), r_k=vec(r_k), lnx_g=vec(lnx_g), lnx_b=vec(lnx_b),
        sinks=sinks, w_out=w_out.astype(BF16), ffn_norm=vec(ffn_norm),
        w_gate=w_gate.astype(BF16), w_up=w_up.astype(BF16), w_down=w_down.astype(BF16))


def _last_pa_row(rkv, lo, nb):
    rkv_last = rkv.reshape(nb, -1, RKV_W)[:, -1]
    lo_last = lo.reshape(nb, -1, LORA_W)[:, -1]
    return jnp.concatenate([rkv_last, _unpad_lora_cols(lo_last)], axis=-1)


def kernel(x_prompt, x_sample, state_rwkv, state_shift, cache_k_win, cache_v_win, attn_norm, w_in, mu, w0, w_decay_up, a0, w_a_up, w_g_up, k_k, k_a, r_k, lnx_g, lnx_b, sinks, w_out, ffn_norm, w_gate, w_up, w_down, final_norm):
    bp, tp, _ = x_prompt.shape
    bs, ts, _ = x_sample.shape
    depth = w_in.shape[0]
    assert ts == 1, "sample kernels handle exactly one new token per sequence"
    tm_p = 512 if tp % 512 == 0 else tp
    tb = 128
    tf = 256
    sample_bb = 8

    cos_p, sin_p = _rope_tables(jnp.arange(tp, dtype=jnp.int32))
    cos_s, sin_s = _rope_tables(jnp.full((bs,), PAST_LEN, dtype=jnp.int32))
    final_g = final_norm.reshape(1, -1)

    wts = _prep_weights(attn_norm, w_in, mu, w0, w_decay_up, a0, w_a_up, w_g_up, k_k, k_a, r_k, lnx_g, lnx_b, sinks,
                        w_out, ffn_norm, w_gate, w_up, w_down)
    prev_rkv = state_shift[..., 0:RKV_W]
    prev_lo = _pad_lora_cols(state_shift[..., RKV_W:])
    s_S = state_rwkv
    s_k = cache_k_win.reshape(depth, bs, WINDOW, KV_WIDTH)
    s_v = cache_v_win.reshape(depth, bs, WINDOW, KV_WIDTH)

    xp = x_prompt.reshape(bp * tp, D_MODEL)
    xs = x_sample.reshape(bs * ts, D_MODEL)
    p_S, p_sh, p_k, p_v, s_sh = [], [], [], [], []
    for l in range(depth):
        final = l == depth - 1

        rkv, lo, q, k, v = _in_proj(xp, wts["attn_norm"], wts["w_all"], l, cos_p, sin_p, tm_p)
        ya, S = _rwkv_chunk(rkv, lo, wts, l, bp, tb)
        yb = _swa_banded(q, k, v, wts["sinks"], l, bp)
        xp = _mix_ffn(xp, ya, yb, wts, l, final_g, final, tm_p, tf)
        p_S.append(S)
        p_sh.append(_last_pa_row(rkv, lo, bp))
        p_k.append(k.reshape(bp, tp, KV_WIDTH)[:, tp - WINDOW:].reshape(bp, WINDOW, N_KV_HEADS, HEAD_DIM))
        p_v.append(v.reshape(bp, tp, KV_WIDTH)[:, tp - WINDOW:].reshape(bp, WINDOW, N_KV_HEADS, HEAD_DIM))

        rkv, lo, q, k, v = _in_proj(xs, wts["attn_norm"], wts["w_all"], l, cos_s, sin_s, bs)
        ya, s_S = _rwkv_step(rkv, lo, prev_rkv, prev_lo, s_S, wts, l, sample_bb)
        yb, s_k, s_v = _swa_step(q, k, v, s_k, s_v, wts["sinks"], l, sample_bb)
        xs = _mix_ffn(xs, ya, yb, wts, l, final_g, final, bs, tf)
        s_sh.append(_last_pa_row(rkv, lo, bs))

    cache_shape = (depth, bs, WINDOW, N_KV_HEADS, HEAD_DIM)
    return (xp.reshape(bp, tp, D_MODEL), xs.reshape(bs, ts, D_MODEL),
            jnp.stack(p_S), jnp.stack(p_sh), jnp.stack(p_k), jnp.stack(p_v),
            s_S, jnp.stack(s_sh), s_k.reshape(cache_shape), s_v.reshape(cache_shape))
```

```python
import functools

import jax
import jax.numpy as jnp
from jax import lax
from jax.experimental import pallas as pl
from jax.experimental.pallas import tpu as pltpu

F32 = jnp.float32
BF16 = jnp.bfloat16

D_MODEL = 1024
HEAD_DIM = 64
N_A_HEADS = 8
A_WIDTH = N_A_HEADS * HEAD_DIM
N_Q_HEADS = 8
N_KV_HEADS = 2
GQA_GROUP = N_Q_HEADS // N_KV_HEADS
B_WIDTH = N_Q_HEADS * HEAD_DIM
KV_WIDTH = N_KV_HEADS * HEAD_DIM
DECAY_LORA = 64
AAA_LORA = 64
GATE_LORA = 160
A_PROJ = 3 * A_WIDTH + DECAY_LORA + AAA_LORA + GATE_LORA
WINDOW = 128
PAST_LEN = 8192
D_FF = 2816
ROPE_THETA = 10000.0
NORM_EPS = 1e-5
LNX_EPS = 64e-5
ATTN_SCALE = HEAD_DIM ** -0.5
EXP_NEG_HALF = 0.6065306597126334

LANES = 128
HEAD_PAIRS = N_A_HEADS // 2
CHUNK = 64
RKV_W = 3 * A_WIDTH
LORA_W = 512
LORA_A_OFF = 128
LORA_G_OFF = 256
PROJ_W = RKV_W + LORA_W + B_WIDTH + 2 * KV_WIDTH
VMEM_LIMIT = 48 * 1024 * 1024


def _dot(a, b):
    return jnp.dot(a, b, preferred_element_type=F32)


def _dot_nt(a, b):
    return lax.dot_general(a, b, (((1,), (1,)), ((), ())), preferred_element_type=F32)


def _iota(shape, dim):
    return lax.broadcasted_iota(jnp.int32, shape, dim)


def _rms(x, g):
    return x * lax.rsqrt(jnp.mean(x * x, axis=-1, keepdims=True) + NORM_EPS) * g


def _bdot(a, b):
    return jnp.dot(a.astype(BF16), b.astype(BF16), preferred_element_type=F32)


def _bdot_nt(a, b):
    return lax.dot_general(a.astype(BF16), b.astype(BF16), (((1,), (1,)), ((), ())), preferred_element_type=F32)


def _split_bf16(x, parts):
    out = []
    for _ in range(parts):
        piece = x.astype(BF16)
        out.append(piece)
        x = x - piece.astype(F32)
    return out


def _head_sum(x):
    lo_lanes = _iota((1, LANES), 1) < HEAD_DIM
    lo_sum = jnp.sum(jnp.where(lo_lanes, x, 0.0), axis=-1, keepdims=True)
    hi_sum = jnp.sum(jnp.where(lo_lanes, 0.0, x), axis=-1, keepdims=True)
    return jnp.where(lo_lanes, lo_sum, hi_sum)


def _params(sem):
    return pltpu.CompilerParams(dimension_semantics=sem, vmem_limit_bytes=VMEM_LIMIT)


def _layer_spec(a, l, resident=False):
    zeros = (0,) * (a.ndim - 1)
    mode = dict(pipeline_mode=pl.Buffered(1)) if resident else {}
    return pl.BlockSpec((None,) + a.shape[1:], lambda *_: (l,) + zeros, **mode)


def _inproj_kernel(x_ref, g_ref, w_ref, cos_ref, sin_ref, rkv_ref, lo_ref, q_ref, k_ref, v_ref):
    hb = _rms(x_ref[...], g_ref[...]).astype(BF16)
    rkv_ref[...] = _dot(hb, w_ref[:, 0:RKV_W])
    lo_ref[...] = _dot(hb, w_ref[:, RKV_W:RKV_W + LORA_W])
    qkv = _dot(hb, w_ref[:, RKV_W + LORA_W:PROJ_W])
    cos = cos_ref[...]
    sin = sin_ref[...]
    first_half = (_iota((1, LANES), 1) % HEAD_DIM) < (HEAD_DIM // 2)

    def rope(t):
        rot = jnp.where(first_half, pltpu.roll(t, LANES - HEAD_DIM // 2, 1), pltpu.roll(t, HEAD_DIM // 2, 1))
        return t * cos + rot * sin

    for j in range(B_WIDTH // LANES):
        q_ref[:, j * LANES:(j + 1) * LANES] = rope(qkv[:, j * LANES:(j + 1) * LANES])
    k_ref[...] = rope(qkv[:, B_WIDTH:B_WIDTH + KV_WIDTH])
    v_ref[...] = qkv[:, B_WIDTH + KV_WIDTH:]


def _in_proj(x, gamma, w_all, l, cos, sin, tm):
    n = x.shape[0]
    nrope = cos.shape[0] // tm
    row = lambda w: pl.BlockSpec((tm, w), lambda i: (i, 0))
    rope_row = pl.BlockSpec((tm, LANES), lambda i: (i % nrope, 0))
    sds = lambda w: jax.ShapeDtypeStruct((n, w), F32)
    return pl.pallas_call(
        _inproj_kernel, grid=(n // tm,),
        in_specs=[row(D_MODEL), _layer_spec(gamma, l), _layer_spec(w_all, l, resident=True), rope_row, rope_row],
        out_specs=[row(RKV_W), row(LORA_W), row(B_WIDTH), row(KV_WIDTH), row(KV_WIDTH)],
        out_shape=[sds(RKV_W), sds(LORA_W), sds(B_WIDTH), sds(KV_WIDTH), sds(KV_WIDTH)],
        compiler_params=_params(("parallel",)), name="in_proj",
    )(x, gamma, w_all, cos, sin)


def _rwkv_prep(xs_rkv, xs_lo, w0, wd, a0, wa, wg, k_k, k_a):
    r = xs_rkv[:, 0:A_WIDTH]
    k = xs_rkv[:, A_WIDTH:2 * A_WIDTH]
    v = xs_rkv[:, 2 * A_WIDTH:3 * A_WIDTH]
    z = w0 + _bdot(jnp.tanh(xs_lo[:, 0:LORA_A_OFF]), wd)
    logw = -EXP_NEG_HALF * jax.nn.sigmoid(z)
    a = jax.nn.sigmoid(a0 + _bdot(xs_lo[:, LORA_A_OFF:LORA_G_OFF], wa))
    g = _bdot(jax.nn.sigmoid(xs_lo[:, LORA_G_OFF:LORA_W]), wg)
    kk = k * k_k
    parts = []
    for j in range(A_WIDTH // LANES):
        t = kk[:, j * LANES:(j + 1) * LANES]
        nrm = jnp.sqrt(_head_sum(t * t))
        parts.append(t / jnp.maximum(nrm, 1e-12))
    kk = jnp.concatenate(parts, axis=1)
    k_mod = k * (1.0 + (a - 1.0) * k_a)
    return r, logw, k_mod, v, -kk, kk * a, g


def _rwkv_post(y, r, k_mod, v, g, r_k, lnx_g, lnx_b):
    parts = []
    for j in range(A_WIDTH // LANES):
        ln = slice(j * LANES, (j + 1) * LANES)
        yj = y[:, ln]
        mean = _head_sum(yj) * (1.0 / HEAD_DIM)
        d = yj - mean
        var = _head_sum(d * d) * (1.0 / HEAD_DIM)
        yn = d * lax.rsqrt(var + LNX_EPS) * lnx_g[:, ln] + lnx_b[:, ln]
        bonus = _head_sum(r[:, ln] * k_mod[:, ln] * r_k[:, ln]) * v[:, ln]
        parts.append((yn + bonus) * g[:, ln])
    return jnp.concatenate(parts, axis=1)


def _rwkv_chunk_kernel(rkv_ref, lo_ref, mu_rkv_ref, mu_lo_ref, w0_ref, wd_ref, a0_ref, wa_ref, wg_ref,
                       kk_ref, ka_ref, rk_ref, lng_ref, lnb_ref,
                       ya_ref, s_out_ref,
                       s_scr, carry_rkv, carry_lo, r_s, lw_s, km_s, v_s, an_s, bn_s, g_s, cum_s, y_s, *, nb, tb, nt):
    t = pl.program_id(0)

    @pl.when(t == 0)
    def _():
        for ref in (s_scr, carry_rkv, carry_lo, r_s, lw_s, km_s, v_s, an_s, bn_s, g_s, cum_s):
            ref[...] = jnp.zeros_like(ref)

    slot = t % 2
    done = 1 - slot

    row0 = _iota((tb, 1), 0) == 0
    ti = _iota((tb, tb), 0)
    tj = _iota((tb, tb), 1)
    tri = ((ti // CHUNK == tj // CHUNK) & (ti >= tj)).astype(BF16)
    for b in range(nb):
        rkv = rkv_ref[b]
        lo = lo_ref[b]
        last = slice(8 * b + 7, 8 * b + 8)
        prev_rkv = jnp.where(row0, carry_rkv[last, :], pltpu.roll(rkv, 1, 0))
        prev_lo = jnp.where(row0, carry_lo[last, :], pltpu.roll(lo, 1, 0))
        carry_rkv[8 * b:8 * b + 8, :] = rkv[tb - 8:tb, :]
        carry_lo[8 * b:8 * b + 8, :] = lo[tb - 8:tb, :]
        xs_rkv = rkv + mu_rkv_ref[...] * (prev_rkv - rkv)
        xs_lo = lo + mu_lo_ref[...] * (prev_lo - lo)
        r, logw, k_mod, v, a_neg, b_pos, g = _rwkv_prep(
            xs_rkv, xs_lo, w0_ref[...], wd_ref[...], a0_ref[...], wa_ref[...], wg_ref[...], kk_ref[...], ka_ref[...])
        blk = slice(b * tb, (b + 1) * tb)
        r_s[slot, blk, :] = r
        lw_s[slot, blk, :] = logw
        km_s[slot, blk, :] = k_mod
        v_s[slot, blk, :] = v
        an_s[slot, blk, :] = a_neg
        bn_s[slot, blk, :] = b_pos
        g_s[slot, blk, :] = g
        cum_s[slot, blk, :] = sum(_dot(tri, piece) for piece in _split_bf16(logw, 3))

    ri = _iota((LANES, LANES), 0)
    rj = _iota((LANES, LANES), 1)
    same = (ri // CHUNK) == (rj // CHUNK)
    strict = same & (ri > rj)
    incl = same & (ri >= rj)
    incl2 = jnp.concatenate([incl, incl], axis=1)
    eye = (ri == rj).astype(F32)
    lo_lanes = _iota((1, LANES), 1) < HEAD_DIM

    def stack(x):
        return jnp.concatenate([jnp.where(lo_lanes, x, 0.0), jnp.where(lo_lanes, 0.0, x)], axis=0)

    pairs = range(nb * HEAD_PAIRS)
    lanes = [slice((p % HEAD_PAIRS) * LANES, (p % HEAD_PAIRS + 1) * LANES) for p in pairs]
    s2 = [s_scr[p] for p in pairs]
    for c in range(tb // CHUNK):
        rowss = [slice((p // HEAD_PAIRS) * tb + c * CHUNK, (p // HEAD_PAIRS) * tb + (c + 1) * CHUNK) for p in pairs]
        ar, bk_end, v2, w_end, gram = [], [], [], [], []
        for p in pairs:
            ln = lanes[p]
            rows = rowss[p]
            lw = lw_s[done, rows, ln]
            cum = cum_s[done, rows, ln]
            w_inv = jnp.exp(-cum)
            w_end.append(jnp.exp(cum[CHUNK - 1:CHUNK, :]))
            a2 = stack(an_s[done, rows, ln] * jnp.exp(cum - lw))
            r2 = stack(r_s[done, rows, ln] * jnp.exp(cum))
            b2 = stack(bn_s[done, rows, ln] * w_inv)
            k2 = stack(km_s[done, rows, ln] * w_inv)
            v2.append(stack(v_s[done, rows, ln]).astype(BF16))
            ar.append(jnp.concatenate([a2, r2], axis=0).astype(BF16))
            bk = jnp.concatenate([b2, k2], axis=0)
            bk_end.append((bk * w_end[p]).astype(BF16))
            gram.append(_bdot_nt(ar[p], bk))
        a_ab = [jnp.where(strict, gram[p][0:LANES, 0:LANES], 0.0) for p in pairs]
        a_ak = [jnp.where(strict, gram[p][0:LANES, LANES:], 0.0) for p in pairs]
        q_bk = [jnp.where(incl2, gram[p][LANES:, :], 0.0) for p in pairs]
        npow = [_bdot(a_ab[p], a_ab[p]) for p in pairs]
        tinv = [eye + a_ab[p] for p in pairs]
        for _ in range(4):
            both = [_bdot(jnp.concatenate([npow[p], tinv[p]], axis=0), npow[p]) for p in pairs]
            npow = [both[p][0:LANES, :] for p in pairs]
            tinv = [tinv[p] + both[p][LANES:, :] for p in pairs]
        tinv = [tinv[p] + _bdot(tinv[p], npow[p]) for p in pairs]
        akv = [_bdot(a_ak[p], v2[p]) for p in pairs]
        ars = [_bdot_nt(ar[p], s2[p]) for p in pairs]
        u2 = [_bdot(tinv[p], ars[p][0:LANES, :] + akv[p]) for p in pairs]
        uv = [jnp.concatenate([u2[p].astype(BF16), v2[p]], axis=0) for p in pairs]
        y2 = [ars[p][LANES:, :] + _bdot(q_bk[p], uv[p]) for p in pairs]
        uv_t = [jnp.concatenate([u2[p].T, v2[p].astype(F32).T], axis=1) for p in pairs]
        s2 = [s2[p] * w_end[p] + _bdot(uv_t[p], bk_end[p]) for p in pairs]
        for p in pairs:
            y_s[rowss[p], lanes[p]] = y2[p][0:CHUNK, :] + y2[p][CHUNK:, :]
    for p in pairs:
        s_scr[p] = s2[p]

    for b in range(nb):
        blk = slice(b * tb, (b + 1) * tb)
        ya_ref[b] = _rwkv_post(y_s[blk, :], r_s[done, blk, :], km_s[done, blk, :], v_s[done, blk, :],
                               g_s[done, blk, :], rk_ref[...], lng_ref[...], lnb_ref[...])

    @pl.when(t == nt)
    def _():
        for p in pairs:
            b, h = p // HEAD_PAIRS, 2 * (p % HEAD_PAIRS)
            s_out_ref[b, h] = s2[p][0:HEAD_DIM, 0:HEAD_DIM]
            s_out_ref[b, h + 1] = s2[p][HEAD_DIM:, HEAD_DIM:]


_RWKV_PARAMS = ("mu_rkv", "mu_lo", "w0", "wd", "a0", "wa", "wg", "k_k", "k_a", "r_k", "lnx_g", "lnx_b")


def _rwkv_chunk(rkv, lo, wts, l, nb, tb):
    n = rkv.shape[0]
    nt = n // nb // tb
    blk_in = lambda w: pl.BlockSpec((nb, tb, w), lambda t: (0, jnp.minimum(t, nt - 1), 0))
    blk_out = pl.BlockSpec((nb, tb, A_WIDTH), lambda t: (0, jnp.maximum(t - 1, 0), 0))
    consts = [wts[k] for k in _RWKV_PARAMS]
    wide = lambda: pltpu.VMEM((2, nb * tb, A_WIDTH), F32)
    ya, state = pl.pallas_call(
        functools.partial(_rwkv_chunk_kernel, nb=nb, tb=tb, nt=nt), grid=(nt + 1,),
        in_specs=[blk_in(RKV_W), blk_in(LORA_W)] + [_layer_spec(a, l) for a in consts],
        out_specs=[blk_out, pl.BlockSpec((nb, N_A_HEADS, HEAD_DIM, HEAD_DIM), lambda t: (0, 0, 0, 0))],
        out_shape=[jax.ShapeDtypeStruct((nb, n // nb, A_WIDTH), F32),
                   jax.ShapeDtypeStruct((nb, N_A_HEADS, HEAD_DIM, HEAD_DIM), F32)],
        scratch_shapes=[pltpu.VMEM((nb * HEAD_PAIRS, LANES, LANES), F32), pltpu.VMEM((8 * nb, RKV_W), F32),
                        pltpu.VMEM((8 * nb, LORA_W), F32)] + [wide() for _ in range(8)]
        + [pltpu.VMEM((nb * tb, A_WIDTH), F32)],
        compiler_params=_params(("arbitrary",)), name="rwkv_chunk",
    )(rkv.reshape(nb, n // nb, RKV_W), lo.reshape(nb, n // nb, LORA_W), *consts)
    return ya.reshape(n, A_WIDTH), state


def _rwkv_step_kernel(rkv_ref, lo_ref, prkv_ref, plo_ref, s_ref, mu_rkv_ref, mu_lo_ref, w0_ref, wd_ref, a0_ref,
                      wa_ref, wg_ref, kk_ref, ka_ref, rk_ref, lng_ref, lnb_ref, *rest, bb):
    ya_ref, s_out_ref, r_s, w_s, km_s, v_s, an_s, bn_s, y_s, y_odd_s = rest[-10:]
    rkv = rkv_ref[...]
    lo = lo_ref[...]
    xs_rkv = rkv + mu_rkv_ref[...] * (prkv_ref[...] - rkv)
    xs_lo = lo + mu_lo_ref[...] * (plo_ref[...] - lo)
    r, logw, k_mod, v, a_neg, b_pos, g = _rwkv_prep(
        xs_rkv, xs_lo, w0_ref[...], wd_ref[...], a0_ref[...], wa_ref[...], wg_ref[...], kk_ref[...], ka_ref[...])
    for ref, val in ((r_s, r), (w_s, jnp.exp(logw)), (km_s, k_mod), (v_s, v), (an_s, a_neg), (bn_s, b_pos)):
        ref[:, 0:A_WIDTH] = val
        ref[:, A_WIDTH:] = pltpu.roll(val, HEAD_DIM, 1)
    y_s[...] = jnp.zeros_like(y_s)
    y_odd_s[...] = jnp.zeros_like(y_odd_s)
    eye = _iota((HEAD_DIM, HEAD_DIM), 0) == _iota((HEAD_DIM, HEAD_DIM), 1)

    def vec(ref, bi, h):
        off = h * HEAD_DIM if h % 2 == 0 else A_WIDTH + ((h + 1) * HEAD_DIM) % A_WIDTH
        return ref[bi:bi + 1, off:off + HEAD_DIM]

    heads = range(N_A_HEADS)
    for bi in range(bb):
        s0 = [s_ref[bi, h] for h in heads]
        sa = [jnp.sum(s0[h] * vec(an_s, bi, h), axis=-1, keepdims=True) for h in heads]
        v_col = [jnp.sum(jnp.where(eye, vec(v_s, bi, h), 0.0), axis=-1, keepdims=True) for h in heads]
        s1 = [s0[h] * vec(w_s, bi, h) + sa[h] * vec(bn_s, bi, h) + v_col[h] * vec(km_s, bi, h) for h in heads]
        for h in heads:
            s_out_ref[bi, h] = s1[h]
        y_col = [jnp.sum(s1[h] * vec(r_s, bi, h), axis=-1, keepdims=True) for h in heads]
        for h in heads:
            y_row = jnp.sum(jnp.where(eye, y_col[h], 0.0), axis=0, keepdims=True)
            half = y_s if h % 2 == 0 else y_odd_s
            half[bi:bi + 1, (h // 2) * LANES:(h // 2) * LANES + HEAD_DIM] = y_row
    lo_lanes = _iota((1, A_WIDTH), 1) % LANES < HEAD_DIM
    y = jnp.where(lo_lanes, y_s[...], pltpu.roll(y_odd_s[...], HEAD_DIM, 1))
    ya_ref[...] = _rwkv_post(y, r, k_mod, v, g, rk_ref[...], lng_ref[...], lnb_ref[...])


def _rwkv_step(rkv, lo, prev_rkv, prev_lo, state, new_state, wts, l, bb):
    n = rkv.shape[0]
    row = lambda w: pl.BlockSpec((bb, w), lambda i: (i, 0))
    prev = lambda w: pl.BlockSpec((None, bb, w), lambda i: (l, i, 0))
    st = pl.BlockSpec((None, bb, N_A_HEADS, HEAD_DIM, HEAD_DIM), lambda i: (l, i, 0, 0, 0))
    consts = [wts[k] for k in _RWKV_PARAMS]
    return pl.pallas_call(
        functools.partial(_rwkv_step_kernel, bb=bb), grid=(n // bb,),
        in_specs=[row(RKV_W), row(LORA_W), prev(RKV_W), prev(LORA_W), st] + [_layer_spec(a, l) for a in consts]
        + ([] if new_state is None else [pl.BlockSpec(memory_space=pl.ANY)]),
        out_specs=[row(A_WIDTH), st],
        out_shape=[jax.ShapeDtypeStruct((n, A_WIDTH), F32), jax.ShapeDtypeStruct(state.shape, F32)],
        input_output_aliases={} if new_state is None else {5 + len(consts): 1},
        scratch_shapes=[pltpu.VMEM((bb, 2 * A_WIDTH), F32) for _ in range(6)]
        + [pltpu.VMEM((bb, A_WIDTH), F32) for _ in range(2)],
        compiler_params=_params(("parallel",)), name="rwkv_step",
    )(rkv, lo, prev_rkv, prev_lo, state, *consts, *([] if new_state is None else [new_state]))


def _dup_kv(x):
    lo_lanes = _iota((1, LANES), 1) < HEAD_DIM
    xr = pltpu.roll(x, HEAD_DIM, 1)
    return [jnp.where(lo_lanes, x, xr), jnp.where(lo_lanes, xr, x)]


def _sink_attention(q, kdup, vdup, sink_ref, l, mask):
    rows = q.shape[0]
    lo_lanes = _iota((1, LANES), 1) < HEAD_DIM
    head_of_row = _iota((GQA_GROUP * rows, 1), 0) // rows
    mask4 = jnp.concatenate([mask] * GQA_GROUP, axis=0)
    groups = range(N_KV_HEADS)
    qs, sink = [], []
    for g in groups:
        parts = []
        sink_g = jnp.zeros((GQA_GROUP * rows, 1), F32)
        for r in range(GQA_GROUP):
            h = g * GQA_GROUP + r
            q128 = q[:, (h // 2) * LANES:(h // 2 + 1) * LANES]
            parts.append(jnp.where(lo_lanes, q128, 0.0) if h % 2 == 0 else jnp.where(lo_lanes, 0.0, q128))
            sink_g = jnp.where(head_of_row == r, sink_ref[l, h], sink_g)
        qs.append(jnp.concatenate(parts, axis=0))
        sink.append(sink_g)
    s = [jnp.where(mask4, _bdot_nt(qs[g], kdup[g]) * ATTN_SCALE, -jnp.inf) for g in groups]
    m = [jnp.maximum(jnp.max(s[g], axis=-1, keepdims=True), sink[g]) for g in groups]
    p = [jnp.exp(s[g] - m[g]) for g in groups]
    den = [jnp.sum(p[g], axis=-1, keepdims=True) + jnp.exp(sink[g] - m[g]) for g in groups]
    o = [_bdot(p[g], vdup[g]) / den[g] for g in groups]
    outs = []
    for g in groups:
        for jj in range(GQA_GROUP // 2):
            even = o[g][(2 * jj) * rows:(2 * jj + 1) * rows, :]
            odd = o[g][(2 * jj + 1) * rows:(2 * jj + 2) * rows, :]
            outs.append(jnp.where(lo_lanes, even, odd))
    return jnp.concatenate(outs, axis=1)


def _swa_banded_kernel(sink_ref, q_ref, kc_ref, kp_ref, vc_ref, vp_ref, o_ref, *, l):
    n = pl.program_id(1)
    kall = jnp.concatenate([kp_ref[...], kc_ref[...]], axis=0)
    vall = jnp.concatenate([vp_ref[...], vc_ref[...]], axis=0)
    qi = _iota((WINDOW, 2 * WINDOW), 0)
    kj = _iota((WINDOW, 2 * WINDOW), 1)
    diff = qi - kj + WINDOW
    band = (diff >= 0) & (diff < WINDOW)
    masks = [band & ((kj >= WINDOW) | (n > 0)), band]
    for j in range(2):
        keys = slice(j * WINDOW, (j + 2) * WINDOW)
        rows = slice(j * WINDOW, (j + 1) * WINDOW)
        o_ref[rows, :] = _sink_attention(q_ref[rows, :], _dup_kv(kall[keys, :]), _dup_kv(vall[keys, :]),
                                         sink_ref, l, masks[j])


def _swa_banded(q, k, v, sinks, l, nb):
    n = q.shape[0]
    nstep = n // nb // (2 * WINDOW)
    cur = lambda w: pl.BlockSpec((2 * WINDOW, w), lambda b, i: (b * nstep + i, 0))
    prev = lambda w: pl.BlockSpec((WINDOW, w), lambda b, i: (2 * (b * nstep + i) - jnp.minimum(i, 1), 0))
    return pl.pallas_call(
        functools.partial(_swa_banded_kernel, l=l), grid=(nb, nstep),
        in_specs=[pl.BlockSpec(memory_space=pltpu.SMEM), cur(B_WIDTH), cur(KV_WIDTH), prev(KV_WIDTH),
                  cur(KV_WIDTH), prev(KV_WIDTH)],
        out_specs=cur(B_WIDTH),
        out_shape=jax.ShapeDtypeStruct((n, B_WIDTH), F32),
        compiler_params=_params(("parallel", "parallel")), name="swa_banded",
    )(sinks, q, k, k, v, v)


def _swa_step_kernel(sink_ref, q_ref, k_ref, v_ref, kb_ref, vb_ref, o_ref, ko_ref, vo_ref, *, l, bb):
    last = _iota((WINDOW, 1), 0) == WINDOW - 1
    head = _iota((N_Q_HEADS, 1), 0)
    lane = _iota((1, LANES), 1)
    own_half = (lane < HEAD_DIM) == (head % 2 == 0)
    own_keys = (_iota((1, 2 * WINDOW), 1) // WINDOW) == (head // GQA_GROUP)
    sink = jnp.zeros((N_Q_HEADS, 1), F32)
    for h in range(N_Q_HEADS):
        sink = jnp.where(head == h, sink_ref[l, h], sink)
    lo_lanes = lane < HEAD_DIM

    rows = range(bb)
    kcat, vcat, q8 = [], [], []
    for bi in rows:
        kc = jnp.where(last, k_ref[bi:bi + 1, :], pltpu.roll(kb_ref[bi], WINDOW - 1, 0))
        vc = jnp.where(last, v_ref[bi:bi + 1, :], pltpu.roll(vb_ref[bi], WINDOW - 1, 0))
        ko_ref[bi] = kc
        vo_ref[bi] = vc
        kcat.append(jnp.concatenate(_dup_kv(kc), axis=0))
        vcat.append(jnp.concatenate(_dup_kv(vc), axis=0))
        q = q_ref[bi:bi + 1, :]
        qsel = q[:, 0:LANES]
        for j in range(1, B_WIDTH // LANES):
            qsel = jnp.where(head // 2 == j, q[:, j * LANES:(j + 1) * LANES], qsel)
        q8.append(jnp.where(own_half, qsel, 0.0))
    s = [jnp.where(own_keys, _dot_nt(q8[bi], kcat[bi]) * ATTN_SCALE, -jnp.inf) for bi in rows]
    m = [jnp.maximum(jnp.max(s[bi], axis=-1, keepdims=True), sink) for bi in rows]
    p = [jnp.exp(s[bi] - m[bi]) for bi in rows]
    den = [jnp.sum(p[bi], axis=-1, keepdims=True) + jnp.exp(sink - m[bi]) for bi in rows]
    o = [_dot(p[bi], vcat[bi]) / den[bi] for bi in rows]
    for bi in rows:
        for j in range(B_WIDTH // LANES):
            o_ref[bi:bi + 1, j * LANES:(j + 1) * LANES] = jnp.where(
                lo_lanes, o[bi][2 * j:2 * j + 1, :], o[bi][2 * j + 1:2 * j + 2, :])


def _swa_step(q, k, v, kbuf, vbuf, sinks, l, bb):
    n = q.shape[0]
    row = lambda w: pl.BlockSpec((bb, w), lambda i: (i, 0))
    buf = pl.BlockSpec((None, bb, WINDOW, KV_WIDTH), lambda i: (l, i, 0, 0))
    return pl.pallas_call(
        functools.partial(_swa_step_kernel, l=l, bb=bb), grid=(n // bb,),
        in_specs=[pl.BlockSpec(memory_space=pltpu.SMEM), row(B_WIDTH), row(KV_WIDTH), row(KV_WIDTH), buf, buf],
        out_specs=[row(B_WIDTH), buf, buf],
        out_shape=[jax.ShapeDtypeStruct((n, B_WIDTH), F32), jax.ShapeDtypeStruct(kbuf.shape, F32),
                   jax.ShapeDtypeStruct(vbuf.shape, F32)],
        input_output_aliases={4: 1, 5: 2},
        compiler_params=_params(("parallel",)), name="swa_step",
    )(sinks, q, k, v, kbuf, vbuf)


def _mix_ffn_kernel(x_ref, ya_ref, yb_ref, wo_ref, g_ref, wg_ref, wu_ref, wd_ref, fg_ref, o_ref, *, tf, final):
    x = (x_ref[...] + _dot(ya_ref[...].astype(BF16), wo_ref[0:A_WIDTH, :])
         + _dot(yb_ref[...].astype(BF16), wo_ref[A_WIDTH:, :]))
    h = _rms(x, g_ref[...]).astype(BF16)
    o_ref[...] = x
    for f in range(D_FF // tf):
        cols = slice(f * tf, (f + 1) * tf)
        gate = _dot(h, wg_ref[:, cols])
        up = _dot(h, wu_ref[:, cols])
        act = (gate * jax.nn.sigmoid(gate) * up).astype(BF16)
        o_ref[...] += _dot(act, wd_ref[cols, :])
    if final:
        o_ref[...] = _rms(o_ref[...], fg_ref[...])


def _mix_ffn(x, ya, yb, wts, l, final_gamma, final, tm, tf):
    n = x.shape[0]
    row = lambda w: pl.BlockSpec((tm, w), lambda i: (i, 0))
    weights = [wts[k] for k in ("w_out", "ffn_norm", "w_gate", "w_up", "w_down")]
    return pl.pallas_call(
        functools.partial(_mix_ffn_kernel, tf=tf, final=final), grid=(n // tm,),
        in_specs=[row(D_MODEL), row(A_WIDTH), row(B_WIDTH)] + [_layer_spec(a, l, resident=True) for a in weights]
        + [pl.BlockSpec(final_gamma.shape, lambda i: (0, 0))],
        out_specs=row(D_MODEL), out_shape=jax.ShapeDtypeStruct((n, D_MODEL), F32),
        compiler_params=_params(("parallel",)), name="mix_ffn",
    )(x, ya, yb, *weights, final_gamma)


def _pad_lora_cols(t):
    z = lambda n: jnp.zeros(t.shape[:-1] + (n,), t.dtype)
    return jnp.concatenate([t[..., 0:64], z(64), t[..., 64:128], z(64), t[..., 128:288], z(96)], axis=-1)


def _unpad_lora_cols(t):
    return jnp.concatenate([t[..., 0:64], t[..., LORA_A_OFF:LORA_A_OFF + 64],
                            t[..., LORA_G_OFF:LORA_G_OFF + GATE_LORA]], axis=-1)


def _pad_rows(t, rows):
    return jnp.concatenate([t, jnp.zeros(t.shape[:-2] + (rows - t.shape[-2], t.shape[-1]), t.dtype)], axis=-2)


def _rope_tables(positions):
    half = HEAD_DIM // 2
    inv = ROPE_THETA ** (-jnp.arange(half, dtype=F32) / half)
    ang = positions.astype(F32)[:, None] * inv[None, :]
    cos = jnp.cos(ang)
    sin = jnp.sin(ang)
    return jnp.tile(cos, (1, LANES // half)), jnp.tile(jnp.concatenate([-sin, sin], axis=1), (1, LANES // HEAD_DIM))


def _prep_weights(attn_norm, w_in, mu, w0, w_decay_up, a0, w_a_up, w_g_up, k_k, k_a, r_k, lnx_g, lnx_b, sinks,
                  w_out, ffn_norm, w_gate, w_up, w_down):
    depth = w_in.shape[0]
    vec = lambda t: t.reshape(depth, 1, -1)
    w_in = w_in.astype(BF16)
    w_all = jnp.concatenate([w_in[..., 0:RKV_W], _pad_lora_cols(w_in[..., RKV_W:A_PROJ]), w_in[..., A_PROJ:]], axis=-1)
    return dict(
        attn_norm=vec(attn_norm), w_all=w_all,
        mu_rkv=vec(mu[:, 0:RKV_W]), mu_lo=vec(_pad_lora_cols(mu[:, RKV_W:])),
        w0=vec(w0), wd=_pad_rows(w_decay_up, LORA_A_OFF), a0=vec(a0),
        wa=_pad_rows(w_a_up, LORA_G_OFF - LORA_A_OFF), wg=_pad_rows(w_g_up, LORA_W - LORA_G_OFF),
        k_k=vec(k_k), k_a=vec(k_a), r_k=vec(r_k), lnx_g=vec(lnx_g), lnx_b=vec(lnx_b),
        sinks=sinks, w_out=w_out.astype(BF16), ffn_norm=vec(ffn_norm),
        w_gate=w_gate.astype(BF16), w_up=w_up.astype(BF16), w_down=w_down.astype(BF16))


def _last_pa_row(rkv, lo, nb):
    rkv_last = rkv.reshape(nb, -1, RKV_W)[:, -1]
    lo_last = lo.reshape(nb, -1, LORA_W)[:, -1]
    return jnp.concatenate([rkv_last, _unpad_lora_cols(lo_last)], axis=-1)


def kernel(x_prompt, x_sample, state_rwkv, state_shift, cache_k_win, cache_v_win, attn_norm, w_in, mu, w0, w_decay_up, a0, w_a_up, w_g_up, k_k, k_a, r_k, lnx_g, lnx_b, sinks, w_out, ffn_norm, w_gate, w_up, w_down, final_norm):
    bp, tp, _ = x_prompt.shape
    bs, ts, _ = x_sample.shape
    depth = w_in.shape[0]
    assert ts == 1, "sample kernels handle exactly one new token per sequence"
    tm_p = 512 if tp % 512 == 0 else tp
    tb = 256
    tf = 256
    sample_bb = 8

    cos_p, sin_p = _rope_tables(jnp.arange(tp, dtype=jnp.int32))
    cos_s, sin_s = _rope_tables(jnp.full((bs,), PAST_LEN, dtype=jnp.int32))
    final_g = final_norm.reshape(1, -1)

    wts = _prep_weights(attn_norm, w_in, mu, w0, w_decay_up, a0, w_a_up, w_g_up, k_k, k_a, r_k, lnx_g, lnx_b, sinks,
                        w_out, ffn_norm, w_gate, w_up, w_down)
    prev_rkv = state_shift[..., 0:RKV_W]
    prev_lo = _pad_lora_cols(state_shift[..., RKV_W:])
    s_S = None
    s_k = cache_k_win.reshape(depth, bs, WINDOW, KV_WIDTH)
    s_v = cache_v_win.reshape(depth, bs, WINDOW, KV_WIDTH)

    xp = x_prompt.reshape(bp * tp, D_MODEL)
    xs = x_sample.reshape(bs * ts, D_MODEL)
    p_S, p_sh, p_k, p_v, s_sh = [], [], [], [], []
    for l in range(depth):
        final = l == depth - 1

        rkv, lo, q, k, v = _in_proj(xp, wts["attn_norm"], wts["w_all"], l, cos_p, sin_p, tm_p)
        ya, S = _rwkv_chunk(rkv, lo, wts, l, bp, tb)
        yb = _swa_banded(q, k, v, wts["sinks"], l, bp)
        xp = _mix_ffn(xp, ya, yb, wts, l, final_g, final, tm_p, tf)
        p_S.append(S)
        p_sh.append(_last_pa_row(rkv, lo, bp))
        p_k.append(k.reshape(bp, tp, KV_WIDTH)[:, tp - WINDOW:].reshape(bp, WINDOW, N_KV_HEADS, HEAD_DIM))
        p_v.append(v.reshape(bp, tp, KV_WIDTH)[:, tp - WINDOW:].reshape(bp, WINDOW, N_KV_HEADS, HEAD_DIM))

        rkv, lo, q, k, v = _in_proj(xs, wts["attn_norm"], wts["w_all"], l, cos_s, sin_s, bs)
        ya, s_S = _rwkv_step(rkv, lo, prev_rkv, prev_lo, state_rwkv, s_S, wts, l, sample_bb)
        yb, s_k, s_v = _swa_step(q, k, v, s_k, s_v, wts["sinks"], l, sample_bb)
        xs = _mix_ffn(xs, ya, yb, wts, l, final_g, final, bs, tf)
        s_sh.append(_last_pa_row(rkv, lo, bs))

    cache_shape = (depth, bs, WINDOW, N_KV_HEADS, HEAD_DIM)
    return (xp.reshape(bp, tp, D_MODEL), xs.reshape(bs, ts, D_MODEL),
            jnp.stack(p_S), jnp.stack(p_sh), jnp.stack(p_k), jnp.stack(p_v),
            s_S, jnp.stack(s_sh), s_k.reshape(cache_shape), s_v.reshape(cache_shape))
```

```python
import functools

import jax
import jax.numpy as jnp
from jax import lax
from jax.experimental import pallas as pl
from jax.experimental.pallas import tpu as pltpu

F32 = jnp.float32
BF16 = jnp.bfloat16

D_MODEL = 1024
HEAD_DIM = 64
N_A_HEADS = 8
A_WIDTH = N_A_HEADS * HEAD_DIM
N_Q_HEADS = 8
N_KV_HEADS = 2
GQA_GROUP = N_Q_HEADS // N_KV_HEADS
B_WIDTH = N_Q_HEADS * HEAD_DIM
KV_WIDTH = N_KV_HEADS * HEAD_DIM
DECAY_LORA = 64
AAA_LORA = 64
GATE_LORA = 160
A_PROJ = 3 * A_WIDTH + DECAY_LORA + AAA_LORA + GATE_LORA
WINDOW = 128
PAST_LEN = 8192
D_FF = 2816
ROPE_THETA = 10000.0
NORM_EPS = 1e-5
LNX_EPS = 64e-5
ATTN_SCALE = HEAD_DIM ** -0.5
EXP_NEG_HALF = 0.6065306597126334

LANES = 128
HEAD_PAIRS = N_A_HEADS // 2
CHUNK = 64
AHEAD_CHUNKS = 2
RKV_W = 3 * A_WIDTH
LORA_W = 512
LORA_A_OFF = 128
LORA_G_OFF = 256
PROJ_W = RKV_W + LORA_W + B_WIDTH + 2 * KV_WIDTH
VMEM_LIMIT = 48 * 1024 * 1024


def _dot(a, b):
    return jnp.dot(a, b, preferred_element_type=F32)


def _dot_nt(a, b):
    return lax.dot_general(a, b, (((1,), (1,)), ((), ())), preferred_element_type=F32)


def _iota(shape, dim):
    return lax.broadcasted_iota(jnp.int32, shape, dim)


def _rms(x, g):
    return x * lax.rsqrt(jnp.mean(x * x, axis=-1, keepdims=True) + NORM_EPS) * g


def _bdot(a, b):
    return jnp.dot(a.astype(BF16), b.astype(BF16), preferred_element_type=F32)


def _bdot_nt(a, b):
    return lax.dot_general(a.astype(BF16), b.astype(BF16), (((1,), (1,)), ((), ())), preferred_element_type=F32)


def _split_bf16(x, parts):
    out = []
    for _ in range(parts):
        piece = x.astype(BF16)
        out.append(piece)
        x = x - piece.astype(F32)
    return out


def _head_sum(x):
    lo_lanes = _iota((1, LANES), 1) < HEAD_DIM
    lo_sum = jnp.sum(jnp.where(lo_lanes, x, 0.0), axis=-1, keepdims=True)
    hi_sum = jnp.sum(jnp.where(lo_lanes, 0.0, x), axis=-1, keepdims=True)
    return jnp.where(lo_lanes, lo_sum, hi_sum)


def _params(sem):
    return pltpu.CompilerParams(dimension_semantics=sem, vmem_limit_bytes=VMEM_LIMIT)


def _layer_spec(a, l, resident=False):
    zeros = (0,) * (a.ndim - 1)
    mode = dict(pipeline_mode=pl.Buffered(1)) if resident else {}
    return pl.BlockSpec((None,) + a.shape[1:], lambda *_: (l,) + zeros, **mode)


def _inproj_kernel(x_ref, g_ref, w_ref, cos_ref, sin_ref, rkv_ref, lo_ref, q_ref, k_ref, v_ref):
    hb = _rms(x_ref[...], g_ref[...]).astype(BF16)
    rkv_ref[...] = _dot(hb, w_ref[:, 0:RKV_W])
    lo_ref[...] = _dot(hb, w_ref[:, RKV_W:RKV_W + LORA_W])
    qkv = _dot(hb, w_ref[:, RKV_W + LORA_W:PROJ_W])
    cos = cos_ref[...]
    sin = sin_ref[...]
    first_half = (_iota((1, LANES), 1) % HEAD_DIM) < (HEAD_DIM // 2)

    def rope(t):
        rot = jnp.where(first_half, pltpu.roll(t, LANES - HEAD_DIM // 2, 1), pltpu.roll(t, HEAD_DIM // 2, 1))
        return t * cos + rot * sin

    for j in range(B_WIDTH // LANES):
        q_ref[:, j * LANES:(j + 1) * LANES] = rope(qkv[:, j * LANES:(j + 1) * LANES])
    k_ref[...] = rope(qkv[:, B_WIDTH:B_WIDTH + KV_WIDTH])
    v_ref[...] = qkv[:, B_WIDTH + KV_WIDTH:]


def _in_proj(x, gamma, w_all, l, cos, sin, tm):
    n = x.shape[0]
    nrope = cos.shape[0] // tm
    row = lambda w: pl.BlockSpec((tm, w), lambda i: (i, 0))
    rope_row = pl.BlockSpec((tm, LANES), lambda i: (i % nrope, 0))
    sds = lambda w: jax.ShapeDtypeStruct((n, w), F32)
    return pl.pallas_call(
        _inproj_kernel, grid=(n // tm,),
        in_specs=[row(D_MODEL), _layer_spec(gamma, l), _layer_spec(w_all, l, resident=True), rope_row, rope_row],
        out_specs=[row(RKV_W), row(LORA_W), row(B_WIDTH), row(KV_WIDTH), row(KV_WIDTH)],
        out_shape=[sds(RKV_W), sds(LORA_W), sds(B_WIDTH), sds(KV_WIDTH), sds(KV_WIDTH)],
        compiler_params=_params(("parallel",)), name="in_proj",
    )(x, gamma, w_all, cos, sin)


def _rwkv_prep(xs_rkv, xs_lo, w0, wd, a0, wa, wg, k_k, k_a):
    r = xs_rkv[:, 0:A_WIDTH]
    k = xs_rkv[:, A_WIDTH:2 * A_WIDTH]
    v = xs_rkv[:, 2 * A_WIDTH:3 * A_WIDTH]
    z = w0 + _bdot(jnp.tanh(xs_lo[:, 0:LORA_A_OFF]), wd)
    logw = -EXP_NEG_HALF * jax.nn.sigmoid(z)
    a = jax.nn.sigmoid(a0 + _bdot(xs_lo[:, LORA_A_OFF:LORA_G_OFF], wa))
    g = _bdot(jax.nn.sigmoid(xs_lo[:, LORA_G_OFF:LORA_W]), wg)
    kk = k * k_k
    parts = []
    for j in range(A_WIDTH // LANES):
        t = kk[:, j * LANES:(j + 1) * LANES]
        nrm = jnp.sqrt(_head_sum(t * t))
        parts.append(t / jnp.maximum(nrm, 1e-12))
    kk = jnp.concatenate(parts, axis=1)
    k_mod = k * (1.0 + (a - 1.0) * k_a)
    return r, logw, k_mod, v, -kk, kk * a, g


def _rwkv_post(y, r, k_mod, v, g, r_k, lnx_g, lnx_b):
    parts = []
    for j in range(A_WIDTH // LANES):
        ln = slice(j * LANES, (j + 1) * LANES)
        yj = y[:, ln]
        mean = _head_sum(yj) * (1.0 / HEAD_DIM)
        d = yj - mean
        var = _head_sum(d * d) * (1.0 / HEAD_DIM)
        yn = d * lax.rsqrt(var + LNX_EPS) * lnx_g[:, ln] + lnx_b[:, ln]
        bonus = _head_sum(r[:, ln] * k_mod[:, ln] * r_k[:, ln]) * v[:, ln]
        parts.append((yn + bonus) * g[:, ln])
    return jnp.concatenate(parts, axis=1)


def _rwkv_chunk_kernel(rkv_ref, lo_ref, mu_rkv_ref, mu_lo_ref, w0_ref, wd_ref, a0_ref, wa_ref, wg_ref,
                       kk_ref, ka_ref, rk_ref, lng_ref, lnb_ref,
                       ya_ref, s_out_ref,
                       s_scr, carry_rkv, carry_lo, r_s, lw_s, km_s, v_s, an_s, bn_s, g_s, cum_s, y_s, *, nb, tb, nt):
    t = pl.program_id(0)

    @pl.when(t == 0)
    def _():
        for ref in (s_scr, carry_rkv, carry_lo, r_s, lw_s, km_s, v_s, an_s, bn_s, g_s, cum_s):
            ref[...] = jnp.zeros_like(ref)

    slot = t % 2
    done = 1 - slot

    row0 = _iota((tb, 1), 0) == 0
    ti = _iota((tb, tb), 0)
    tj = _iota((tb, tb), 1)
    tri = ((ti // CHUNK == tj // CHUNK) & (ti >= tj)).astype(BF16)
    for b in range(nb):
        rkv = rkv_ref[b]
        lo = lo_ref[b]
        last = slice(8 * b + 7, 8 * b + 8)
        prev_rkv = jnp.where(row0, carry_rkv[last, :], pltpu.roll(rkv, 1, 0))
        prev_lo = jnp.where(row0, carry_lo[last, :], pltpu.roll(lo, 1, 0))
        carry_rkv[8 * b:8 * b + 8, :] = rkv[tb - 8:tb, :]
        carry_lo[8 * b:8 * b + 8, :] = lo[tb - 8:tb, :]
        xs_rkv = rkv + mu_rkv_ref[...] * (prev_rkv - rkv)
        xs_lo = lo + mu_lo_ref[...] * (prev_lo - lo)
        r, logw, k_mod, v, a_neg, b_pos, g = _rwkv_prep(
            xs_rkv, xs_lo, w0_ref[...], wd_ref[...], a0_ref[...], wa_ref[...], wg_ref[...], kk_ref[...], ka_ref[...])
        blk = slice(b * tb, (b + 1) * tb)
        r_s[slot, blk, :] = r
        lw_s[slot, blk, :] = logw
        km_s[slot, blk, :] = k_mod
        v_s[slot, blk, :] = v
        an_s[slot, blk, :] = a_neg
        bn_s[slot, blk, :] = b_pos
        g_s[slot, blk, :] = g
        cum_s[slot, blk, :] = sum(_dot(tri, piece) for piece in _split_bf16(logw, 3))

    ri = _iota((LANES, LANES), 0)
    rj = _iota((LANES, LANES), 1)
    same = (ri // CHUNK) == (rj // CHUNK)
    strict = same & (ri > rj)
    incl = same & (ri >= rj)
    incl2 = jnp.concatenate([incl, incl], axis=1)
    eye = (ri == rj).astype(F32)
    lo_lanes = _iota((1, LANES), 1) < HEAD_DIM

    def stack(x):
        return jnp.concatenate([jnp.where(lo_lanes, x, 0.0), jnp.where(lo_lanes, 0.0, x)], axis=0)

    pairs = range(nb * HEAD_PAIRS)
    lanes = [slice((p % HEAD_PAIRS) * LANES, (p % HEAD_PAIRS + 1) * LANES) for p in pairs]
    s2 = [s_scr[p] for p in pairs]
    for c0 in range(0, tb // CHUNK, AHEAD_CHUNKS):
        items = [(c, p) for c in range(c0, c0 + AHEAD_CHUNKS) for p in pairs]
        rowss = {(c, p): slice((p // HEAD_PAIRS) * tb + c * CHUNK, (p // HEAD_PAIRS) * tb + (c + 1) * CHUNK)
                 for c, p in items}
        ar, bk_end, v2, w_end, gram = {}, {}, {}, {}, {}
        for it in items:
            ln = lanes[it[1]]
            rows = rowss[it]
            lw = lw_s[done, rows, ln]
            cum = cum_s[done, rows, ln]
            w_inv = jnp.exp(-cum)
            w_end[it] = jnp.exp(cum[CHUNK - 1:CHUNK, :])
            a2 = stack(an_s[done, rows, ln] * jnp.exp(cum - lw))
            r2 = stack(r_s[done, rows, ln] * jnp.exp(cum))
            b2 = stack(bn_s[done, rows, ln] * w_inv)
            k2 = stack(km_s[done, rows, ln] * w_inv)
            v2[it] = stack(v_s[done, rows, ln]).astype(BF16)
            ar[it] = jnp.concatenate([a2, r2], axis=0).astype(BF16)
            bk = jnp.concatenate([b2, k2], axis=0)
            bk_end[it] = (bk * w_end[it]).astype(BF16)
            gram[it] = _bdot_nt(ar[it], bk)
        a_ab = {it: jnp.where(strict, gram[it][0:LANES, 0:LANES], 0.0) for it in items}
        a_ak = {it: jnp.where(strict, gram[it][0:LANES, LANES:], 0.0) for it in items}
        q_bk = {it: jnp.where(incl2, gram[it][LANES:, :], 0.0) for it in items}
        npow = {it: _bdot(a_ab[it], a_ab[it]) for it in items}
        tinv = {it: eye + a_ab[it] for it in items}
        for _ in range(4):
            both = {it: _bdot(jnp.concatenate([npow[it], tinv[it]], axis=0), npow[it]) for it in items}
            npow = {it: both[it][0:LANES, :] for it in items}
            tinv = {it: tinv[it] + both[it][LANES:, :] for it in items}
        tinv = {it: tinv[it] + _bdot(tinv[it], npow[it]) for it in items}
        akv = {it: _bdot(a_ak[it], v2[it]) for it in items}
        for c in range(c0, c0 + AHEAD_CHUNKS):
            its = [(c, p) for p in pairs]
            ars = {it: _bdot_nt(ar[it], s2[it[1]]) for it in its}
            u2 = {it: _bdot(tinv[it], ars[it][0:LANES, :] + akv[it]) for it in its}
            uv = {it: jnp.concatenate([u2[it].astype(BF16), v2[it]], axis=0) for it in its}
            y2 = {it: ars[it][LANES:, :] + _bdot(q_bk[it], uv[it]) for it in its}
            uv_t = {it: jnp.concatenate([u2[it].T, v2[it].astype(F32).T], axis=1) for it in its}
            for it in its:
                s2[it[1]] = s2[it[1]] * w_end[it] + _bdot(uv_t[it], bk_end[it])
                y_s[rowss[it], lanes[it[1]]] = y2[it][0:CHUNK, :] + y2[it][CHUNK:, :]
    for p in pairs:
        s_scr[p] = s2[p]

    for b in range(nb):
        blk = slice(b * tb, (b + 1) * tb)
        ya_ref[b] = _rwkv_post(y_s[blk, :], r_s[done, blk, :], km_s[done, blk, :], v_s[done, blk, :],
                               g_s[done, blk, :], rk_ref[...], lng_ref[...], lnb_ref[...])

    @pl.when(t == nt)
    def _():
        for p in pairs:
            b, h = p // HEAD_PAIRS, 2 * (p % HEAD_PAIRS)
            s_out_ref[b, h] = s2[p][0:HEAD_DIM, 0:HEAD_DIM]
            s_out_ref[b, h + 1] = s2[p][HEAD_DIM:, HEAD_DIM:]


_RWKV_PARAMS = ("mu_rkv", "mu_lo", "w0", "wd", "a0", "wa", "wg", "k_k", "k_a", "r_k", "lnx_g", "lnx_b")


def _rwkv_chunk(rkv, lo, wts, l, nb, tb):
    n = rkv.shape[0]
    nt = n // nb // tb
    blk_in = lambda w: pl.BlockSpec((nb, tb, w), lambda t: (0, jnp.minimum(t, nt - 1), 0))
    blk_out = pl.BlockSpec((nb, tb, A_WIDTH), lambda t: (0, jnp.maximum(t - 1, 0), 0))
    consts = [wts[k] for k in _RWKV_PARAMS]
    wide = lambda: pltpu.VMEM((2, nb * tb, A_WIDTH), F32)
    ya, state = pl.pallas_call(
        functools.partial(_rwkv_chunk_kernel, nb=nb, tb=tb, nt=nt), grid=(nt + 1,),
        in_specs=[blk_in(RKV_W), blk_in(LORA_W)] + [_layer_spec(a, l) for a in consts],
        out_specs=[blk_out, pl.BlockSpec((nb, N_A_HEADS, HEAD_DIM, HEAD_DIM), lambda t: (0, 0, 0, 0))],
        out_shape=[jax.ShapeDtypeStruct((nb, n // nb, A_WIDTH), F32),
                   jax.ShapeDtypeStruct((nb, N_A_HEADS, HEAD_DIM, HEAD_DIM), F32)],
        scratch_shapes=[pltpu.VMEM((nb * HEAD_PAIRS, LANES, LANES), F32), pltpu.VMEM((8 * nb, RKV_W), F32),
                        pltpu.VMEM((8 * nb, LORA_W), F32)] + [wide() for _ in range(8)]
        + [pltpu.VMEM((nb * tb, A_WIDTH), F32)],
        compiler_params=_params(("arbitrary",)), name="rwkv_chunk",
    )(rkv.reshape(nb, n // nb, RKV_W), lo.reshape(nb, n // nb, LORA_W), *consts)
    return ya.reshape(n, A_WIDTH), state


I_UNROLL = 8


def _rwkv_step_kernel(rkv_ref, lo_ref, prkv_ref, plo_ref, s_ref, mu_rkv_ref, mu_lo_ref, w0_ref, wd_ref, a0_ref,
                      wa_ref, wg_ref, kk_ref, ka_ref, rk_ref, lng_ref, lnb_ref, *rest):
    ya_ref, s_out_ref, r_t, w_t, km_t, v_t, an_t, bn_t, y_t, r_s, km_s, v_s, g_s = rest[-13:]
    h = pl.program_id(0)

    @pl.when(h == 0)
    def _():
        rkv = rkv_ref[...]
        lo = lo_ref[...]
        xs_rkv = rkv + mu_rkv_ref[...] * (prkv_ref[...] - rkv)
        xs_lo = lo + mu_lo_ref[...] * (plo_ref[...] - lo)
        r, logw, k_mod, v, a_neg, b_pos, g = _rwkv_prep(
            xs_rkv, xs_lo, w0_ref[...], wd_ref[...], a0_ref[...], wa_ref[...], wg_ref[...], kk_ref[...],
            ka_ref[...])
        for ref, val in ((r_t, r), (w_t, jnp.exp(logw)), (km_t, k_mod), (v_t, v), (an_t, a_neg), (bn_t, b_pos)):
            ref[...] = val.T
        r_s[...] = r
        km_s[...] = k_mod
        v_s[...] = v
        g_s[...] = g

    head = pl.ds(pl.multiple_of(h * HEAD_DIM, HEAD_DIM), HEAD_DIM)
    r_h, w_h, k_h, a_h, b_h = r_t[head, :], w_t[head, :], km_t[head, :], an_t[head, :], bn_t[head, :]

    def rows(g_i, carry):
        ids = [g_i * I_UNROLL + u for u in range(I_UNROLL)]
        vrow = [v_t[pl.ds(h * HEAD_DIM + i, 1), :] for i in ids]
        s0 = [s_ref[i] for i in ids]
        sa = [jnp.sum(s0[u] * a_h, axis=0, keepdims=True) for u in range(I_UNROLL)]
        s1 = [s0[u] * w_h + sa[u] * b_h + vrow[u] * k_h for u in range(I_UNROLL)]
        for u, i in enumerate(ids):
            s_out_ref[i] = s1[u]
            y_t[pl.ds(h * HEAD_DIM + i, 1), :] = jnp.sum(s1[u] * r_h, axis=0, keepdims=True)
        return carry

    lax.fori_loop(0, HEAD_DIM // I_UNROLL, rows, 0)

    @pl.when(h == N_A_HEADS - 1)
    def _():
        ya_ref[...] = _rwkv_post(y_t[...].T, r_s[...], km_s[...], v_s[...], g_s[...],
                                 rk_ref[...], lng_ref[...], lnb_ref[...])


def _rwkv_step(rkv, lo, prev_rkv, prev_lo, state_t, new_state_t, wts, l):
    n = rkv.shape[0]
    full = lambda a: pl.BlockSpec(a.shape, lambda h: (0,) * a.ndim)
    prev = lambda w: pl.BlockSpec((None, n, w), lambda h: (l, 0, 0))
    st = pl.BlockSpec((None, None, HEAD_DIM, HEAD_DIM, n), lambda h: (l, h, 0, 0, 0))
    consts = [wts[k] for k in _RWKV_PARAMS]
    return pl.pallas_call(
        _rwkv_step_kernel, grid=(N_A_HEADS,),
        in_specs=[full(rkv), full(lo), prev(RKV_W), prev(LORA_W), st] + [_layer_spec(a, l) for a in consts]
        + ([] if new_state_t is None else [pl.BlockSpec(memory_space=pl.ANY)]),
        out_specs=[pl.BlockSpec((n, A_WIDTH), lambda h: (0, 0)), st],
        out_shape=[jax.ShapeDtypeStruct((n, A_WIDTH), F32), jax.ShapeDtypeStruct(state_t.shape, F32)],
        input_output_aliases={} if new_state_t is None else {5 + len(consts): 1},
        scratch_shapes=[pltpu.VMEM((A_WIDTH, n), F32) for _ in range(7)]
        + [pltpu.VMEM((n, A_WIDTH), F32) for _ in range(4)],
        compiler_params=_params(("arbitrary",)), name="rwkv_step",
    )(rkv, lo, prev_rkv, prev_lo, state_t, *consts, *([] if new_state_t is None else [new_state_t]))


def _dup_kv(x):
    lo_lanes = _iota((1, LANES), 1) < HEAD_DIM
    xr = pltpu.roll(x, HEAD_DIM, 1)
    return [jnp.where(lo_lanes, x, xr), jnp.where(lo_lanes, xr, x)]


def _sink_attention(q, kdup, vdup, sink_ref, l, mask):
    rows = q.shape[0]
    lo_lanes = _iota((1, LANES), 1) < HEAD_DIM
    head_of_row = _iota((GQA_GROUP * rows, 1), 0) // rows
    mask4 = jnp.concatenate([mask] * GQA_GROUP, axis=0)
    groups = range(N_KV_HEADS)
    qs, sink = [], []
    for g in groups:
        parts = []
        sink_g = jnp.zeros((GQA_GROUP * rows, 1), F32)
        for r in range(GQA_GROUP):
            h = g * GQA_GROUP + r
            q128 = q[:, (h // 2) * LANES:(h // 2 + 1) * LANES] * ATTN_SCALE
            parts.append(jnp.where(lo_lanes, q128, 0.0) if h % 2 == 0 else jnp.where(lo_lanes, 0.0, q128))
            sink_g = jnp.where(head_of_row == r, sink_ref[l, h], sink_g)
        qs.append(jnp.concatenate(parts, axis=0))
        sink.append(sink_g)
    s = [jnp.where(mask4, _bdot_nt(qs[g], kdup[g]), -jnp.inf) for g in groups]
    m = [jnp.maximum(jnp.max(s[g], axis=-1, keepdims=True), sink[g]) for g in groups]
    p = [jnp.exp(s[g] - m[g]) for g in groups]
    den = [jnp.sum(p[g], axis=-1, keepdims=True) + jnp.exp(sink[g] - m[g]) for g in groups]
    o = [_bdot(p[g], vdup[g]) * (1.0 / den[g]) for g in groups]
    outs = []
    for g in groups:
        for jj in range(GQA_GROUP // 2):
            even = o[g][(2 * jj) * rows:(2 * jj + 1) * rows, :]
            odd = o[g][(2 * jj + 1) * rows:(2 * jj + 2) * rows, :]
            outs.append(jnp.where(lo_lanes, even, odd))
    return jnp.concatenate(outs, axis=1)


def _swa_banded_kernel(sink_ref, q_ref, kc_ref, kp_ref, vc_ref, vp_ref, o_ref, *, l):
    n = pl.program_id(1)
    kall = jnp.concatenate([kp_ref[...], kc_ref[...]], axis=0)
    vall = jnp.concatenate([vp_ref[...], vc_ref[...]], axis=0)
    qi = _iota((WINDOW, 2 * WINDOW), 0)
    kj = _iota((WINDOW, 2 * WINDOW), 1)
    diff = qi - kj + WINDOW
    band = (diff >= 0) & (diff < WINDOW)
    masks = [band & ((kj >= WINDOW) | (n > 0)), band]
    for j in range(2):
        keys = slice(j * WINDOW, (j + 2) * WINDOW)
        rows = slice(j * WINDOW, (j + 1) * WINDOW)
        o_ref[rows, :] = _sink_attention(q_ref[rows, :], _dup_kv(kall[keys, :]), _dup_kv(vall[keys, :]),
                                         sink_ref, l, masks[j])


def _swa_banded(q, k, v, sinks, l, nb):
    n = q.shape[0]
    nstep = n // nb // (2 * WINDOW)
    cur = lambda w: pl.BlockSpec((2 * WINDOW, w), lambda b, i: (b * nstep + i, 0))
    prev = lambda w: pl.BlockSpec((WINDOW, w), lambda b, i: (2 * (b * nstep + i) - jnp.minimum(i, 1), 0))
    return pl.pallas_call(
        functools.partial(_swa_banded_kernel, l=l), grid=(nb, nstep),
        in_specs=[pl.BlockSpec(memory_space=pltpu.SMEM), cur(B_WIDTH), cur(KV_WIDTH), prev(KV_WIDTH),
                  cur(KV_WIDTH), prev(KV_WIDTH)],
        out_specs=cur(B_WIDTH),
        out_shape=jax.ShapeDtypeStruct((n, B_WIDTH), F32),
        compiler_params=_params(("parallel", "parallel")), name="swa_banded",
    )(sinks, q, k, k, v, v)


def _swa_step_kernel(sink_ref, q_ref, k_ref, v_ref, kb_ref, vb_ref, o_ref, ko_ref, vo_ref, *, l, bb):
    last = _iota((WINDOW, 1), 0) == WINDOW - 1
    head = _iota((N_Q_HEADS, 1), 0)
    lane = _iota((1, LANES), 1)
    own_half = (lane < HEAD_DIM) == (head % 2 == 0)
    own_keys = (_iota((1, 2 * WINDOW), 1) // WINDOW) == (head // GQA_GROUP)
    sink = jnp.zeros((N_Q_HEADS, 1), F32)
    for h in range(N_Q_HEADS):
        sink = jnp.where(head == h, sink_ref[l, h], sink)
    lo_lanes = lane < HEAD_DIM

    rows = range(bb)
    kcat, vcat, q8 = [], [], []
    for bi in rows:
        kc = jnp.where(last, k_ref[bi:bi + 1, :], pltpu.roll(kb_ref[bi], WINDOW - 1, 0))
        vc = jnp.where(last, v_ref[bi:bi + 1, :], pltpu.roll(vb_ref[bi], WINDOW - 1, 0))
        ko_ref[bi] = kc
        vo_ref[bi] = vc
        kcat.append(jnp.concatenate(_dup_kv(kc), axis=0))
        vcat.append(jnp.concatenate(_dup_kv(vc), axis=0))
        q = q_ref[bi:bi + 1, :]
        qsel = q[:, 0:LANES]
        for j in range(1, B_WIDTH // LANES):
            qsel = jnp.where(head // 2 == j, q[:, j * LANES:(j + 1) * LANES], qsel)
        q8.append(jnp.where(own_half, qsel, 0.0))
    s = [jnp.where(own_keys, _dot_nt(q8[bi], kcat[bi]) * ATTN_SCALE, -jnp.inf) for bi in rows]
    m = [jnp.maximum(jnp.max(s[bi], axis=-1, keepdims=True), sink) for bi in rows]
    p = [jnp.exp(s[bi] - m[bi]) for bi in rows]
    den = [jnp.sum(p[bi], axis=-1, keepdims=True) + jnp.exp(sink - m[bi]) for bi in rows]
    o = [_dot(p[bi], vcat[bi]) / den[bi] for bi in rows]
    for bi in rows:
        for j in range(B_WIDTH // LANES):
            o_ref[bi:bi + 1, j * LANES:(j + 1) * LANES] = jnp.where(
                lo_lanes, o[bi][2 * j:2 * j + 1, :], o[bi][2 * j + 1:2 * j + 2, :])


def _swa_step(q, k, v, kbuf, vbuf, sinks, l, bb):
    n = q.shape[0]
    row = lambda w: pl.BlockSpec((bb, w), lambda i: (i, 0))
    buf = pl.BlockSpec((None, bb, WINDOW, KV_WIDTH), lambda i: (l, i, 0, 0))
    return pl.pallas_call(
        functools.partial(_swa_step_kernel, l=l, bb=bb), grid=(n // bb,),
        in_specs=[pl.BlockSpec(memory_space=pltpu.SMEM), row(B_WIDTH), row(KV_WIDTH), row(KV_WIDTH), buf, buf],
        out_specs=[row(B_WIDTH), buf, buf],
        out_shape=[jax.ShapeDtypeStruct((n, B_WIDTH), F32), jax.ShapeDtypeStruct(kbuf.shape, F32),
                   jax.ShapeDtypeStruct(vbuf.shape, F32)],
        input_output_aliases={4: 1, 5: 2},
        compiler_params=_params(("parallel",)), name="swa_step",
    )(sinks, q, k, v, kbuf, vbuf)


def _mix_ffn_kernel(x_ref, ya_ref, yb_ref, wo_ref, g_ref, wg_ref, wu_ref, wd_ref, fg_ref, o_ref, *, tf, final):
    x = (x_ref[...] + _dot(ya_ref[...].astype(BF16), wo_ref[0:A_WIDTH, :])
         + _dot(yb_ref[...].astype(BF16), wo_ref[A_WIDTH:, :]))
    h = _rms(x, g_ref[...]).astype(BF16)
    o_ref[...] = x
    for f in range(D_FF // tf):
        cols = slice(f * tf, (f + 1) * tf)
        gate = _dot(h, wg_ref[:, cols])
        up = _dot(h, wu_ref[:, cols])
        act = (gate * jax.nn.sigmoid(gate) * up).astype(BF16)
        o_ref[...] += _dot(act, wd_ref[cols, :])
    if final:
        o_ref[...] = _rms(o_ref[...], fg_ref[...])


def _mix_ffn(x, ya, yb, wts, l, final_gamma, final, tm, tf):
    n = x.shape[0]
    row = lambda w: pl.BlockSpec((tm, w), lambda i: (i, 0))
    weights = [wts[k] for k in ("w_out", "ffn_norm", "w_gate", "w_up", "w_down")]
    return pl.pallas_call(
        functools.partial(_mix_ffn_kernel, tf=tf, final=final), grid=(n // tm,),
        in_specs=[row(D_MODEL), row(A_WIDTH), row(B_WIDTH)] + [_layer_spec(a, l, resident=True) for a in weights]
        + [pl.BlockSpec(final_gamma.shape, lambda i: (0, 0))],
        out_specs=row(D_MODEL), out_shape=jax.ShapeDtypeStruct((n, D_MODEL), F32),
        compiler_params=_params(("parallel",)), name="mix_ffn",
    )(x, ya, yb, *weights, final_gamma)


def _pad_lora_cols(t):
    z = lambda n: jnp.zeros(t.shape[:-1] + (n,), t.dtype)
    return jnp.concatenate([t[..., 0:64], z(64), t[..., 64:128], z(64), t[..., 128:288], z(96)], axis=-1)


def _unpad_lora_cols(t):
    return jnp.concatenate([t[..., 0:64], t[..., LORA_A_OFF:LORA_A_OFF + 64],
                            t[..., LORA_G_OFF:LORA_G_OFF + GATE_LORA]], axis=-1)


def _pad_rows(t, rows):
    return jnp.concatenate([t, jnp.zeros(t.shape[:-2] + (rows - t.shape[-2], t.shape[-1]), t.dtype)], axis=-2)


def _rope_tables(positions):
    half = HEAD_DIM // 2
    inv = ROPE_THETA ** (-jnp.arange(half, dtype=F32) / half)
    ang = positions.astype(F32)[:, None] * inv[None, :]
    cos = jnp.cos(ang)
    sin = jnp.sin(ang)
    return jnp.tile(cos, (1, LANES // half)), jnp.tile(jnp.concatenate([-sin, sin], axis=1), (1, LANES // HEAD_DIM))


def _prep_weights(attn_norm, w_in, mu, w0, w_decay_up, a0, w_a_up, w_g_up, k_k, k_a, r_k, lnx_g, lnx_b, sinks,
                  w_out, ffn_norm, w_gate, w_up, w_down):
    depth = w_in.shape[0]
    vec = lambda t: t.reshape(depth, 1, -1)
    w_in = w_in.astype(BF16)
    w_all = jnp.concatenate([w_in[..., 0:RKV_W], _pad_lora_cols(w_in[..., RKV_W:A_PROJ]), w_in[..., A_PROJ:]], axis=-1)
    return dict(
        attn_norm=vec(attn_norm), w_all=w_all,
        mu_rkv=vec(mu[:, 0:RKV_W]), mu_lo=vec(_pad_lora_cols(mu[:, RKV_W:])),
        w0=vec(w0), wd=_pad_rows(w_decay_up, LORA_A_OFF), a0=vec(a0),
        wa=_pad_rows(w_a_up, LORA_G_OFF - LORA_A_OFF), wg=_pad_rows(w_g_up, LORA_W - LORA_G_OFF),
        k_k=vec(k_k), k_a=vec(k_a), r_k=vec(r_k), lnx_g=vec(lnx_g), lnx_b=vec(lnx_b),
        sinks=sinks, w_out=w_out.astype(BF16), ffn_norm=vec(ffn_norm),
        w_gate=w_gate.astype(BF16), w_up=w_up.astype(BF16), w_down=w_down.astype(BF16))


def _last_pa_row(rkv, lo, nb):
    rkv_last = rkv.reshape(nb, -1, RKV_W)[:, -1]
    lo_last = lo.reshape(nb, -1, LORA_W)[:, -1]
    return jnp.concatenate([rkv_last, _unpad_lora_cols(lo_last)], axis=-1)


def kernel(x_prompt, x_sample, state_rwkv, state_shift, cache_k_win, cache_v_win, attn_norm, w_in, mu, w0, w_decay_up, a0, w_a_up, w_g_up, k_k, k_a, r_k, lnx_g, lnx_b, sinks, w_out, ffn_norm, w_gate, w_up, w_down, final_norm):
    bp, tp, _ = x_prompt.shape
    bs, ts, _ = x_sample.shape
    depth = w_in.shape[0]
    assert ts == 1, "sample kernels handle exactly one new token per sequence"
    tm_p = 512 if tp % 512 == 0 else tp
    tm_in = 1024 if tp % 1024 == 0 else tm_p
    tb = 256
    tf = 256
    sample_bb = 8

    cos_p, sin_p = _rope_tables(jnp.arange(tp, dtype=jnp.int32))
    cos_s, sin_s = _rope_tables(jnp.full((bs,), PAST_LEN, dtype=jnp.int32))
    final_g = final_norm.reshape(1, -1)

    wts = _prep_weights(attn_norm, w_in, mu, w0, w_decay_up, a0, w_a_up, w_g_up, k_k, k_a, r_k, lnx_g, lnx_b, sinks,
                        w_out, ffn_norm, w_gate, w_up, w_down)
    prev_rkv = state_shift[..., 0:RKV_W]
    prev_lo = _pad_lora_cols(state_shift[..., RKV_W:])
    state_t = jnp.transpose(state_rwkv, (0, 2, 3, 4, 1))
    s_S = None
    s_k = cache_k_win.reshape(depth, bs, WINDOW, KV_WIDTH)
    s_v = cache_v_win.reshape(depth, bs, WINDOW, KV_WIDTH)

    xp = x_prompt.reshape(bp * tp, D_MODEL)
    xs = x_sample.reshape(bs * ts, D_MODEL)
    p_S, p_sh, p_k, p_v, s_sh = [], [], [], [], []
    for l in range(depth):
        final = l == depth - 1

        rkv, lo, q, k, v = _in_proj(xp, wts["attn_norm"], wts["w_all"], l, cos_p, sin_p, tm_in)
        ya, S = _rwkv_chunk(rkv, lo, wts, l, bp, tb)
        yb = _swa_banded(q, k, v, wts["sinks"], l, bp)
        xp = _mix_ffn(xp, ya, yb, wts, l, final_g, final, tm_p, tf)
        p_S.append(S)
        p_sh.append(_last_pa_row(rkv, lo, bp))
        p_k.append(k.reshape(bp, tp, KV_WIDTH)[:, tp - WINDOW:].reshape(bp, WINDOW, N_KV_HEADS, HEAD_DIM))
        p_v.append(v.reshape(bp, tp, KV_WIDTH)[:, tp - WINDOW:].reshape(bp, WINDOW, N_KV_HEADS, HEAD_DIM))

        rkv, lo, q, k, v = _in_proj(xs, wts["attn_norm"], wts["w_all"], l, cos_s, sin_s, bs)
        ya, s_S = _rwkv_step(rkv, lo, prev_rkv, prev_lo, state_t, s_S, wts, l)
        yb, s_k, s_v = _swa_step(q, k, v, s_k, s_v, wts["sinks"], l, sample_bb)
        xs = _mix_ffn(xs, ya, yb, wts, l, final_g, final, bs, tf)
        s_sh.append(_last_pa_row(rkv, lo, bs))

    cache_shape = (depth, bs, WINDOW, N_KV_HEADS, HEAD_DIM)
    return (xp.reshape(bp, tp, D_MODEL), xs.reshape(bs, ts, D_MODEL),
            jnp.stack(p_S), jnp.stack(p_sh), jnp.stack(p_k), jnp.stack(p_v),
            jnp.transpose(s_S, (0, 4, 1, 2, 3)), jnp.stack(s_sh), s_k.reshape(cache_shape), s_v.reshape(cache_shape))
```

```python
import functools

import jax
import jax.numpy as jnp
from jax import lax
from jax.experimental import pallas as pl
from jax.experimental.pallas import tpu as pltpu

F32 = jnp.float32
BF16 = jnp.bfloat16

D_MODEL = 1024
HEAD_DIM = 64
N_A_HEADS = 8
A_WIDTH = N_A_HEADS * HEAD_DIM
N_Q_HEADS = 8
N_KV_HEADS = 2
GQA_GROUP = N_Q_HEADS // N_KV_HEADS
B_WIDTH = N_Q_HEADS * HEAD_DIM
KV_WIDTH = N_KV_HEADS * HEAD_DIM
DECAY_LORA = 64
AAA_LORA = 64
GATE_LORA = 160
A_PROJ = 3 * A_WIDTH + DECAY_LORA + AAA_LORA + GATE_LORA
WINDOW = 128
PAST_LEN = 8192
D_FF = 2816
ROPE_THETA = 10000.0
NORM_EPS = 1e-5
LNX_EPS = 64e-5
ATTN_SCALE = HEAD_DIM ** -0.5
EXP_NEG_HALF = 0.6065306597126334

LANES = 128
SUBLANES = 8
HEAD_PAIRS = N_A_HEADS // 2
CHUNK = 64
AHEAD_CHUNKS = 2
RKV_W = 3 * A_WIDTH
LORA_W = 512
LORA_A_OFF = 128
LORA_G_OFF = 256
PROJ_W = RKV_W + LORA_W + B_WIDTH + 2 * KV_WIDTH
VMEM_LIMIT = 48 * 1024 * 1024


def _dot(a, b):
    return jnp.dot(a, b, preferred_element_type=F32)


def _dot_nt(a, b):
    return lax.dot_general(a, b, (((1,), (1,)), ((), ())), preferred_element_type=F32)


def _iota(shape, dim):
    return lax.broadcasted_iota(jnp.int32, shape, dim)


def _rms(x, g):
    return x * lax.rsqrt(jnp.mean(x * x, axis=-1, keepdims=True) + NORM_EPS) * g


def _bdot(a, b):
    return jnp.dot(a.astype(BF16), b.astype(BF16), preferred_element_type=F32)


def _bdot_nt(a, b):
    return lax.dot_general(a.astype(BF16), b.astype(BF16), (((1,), (1,)), ((), ())), preferred_element_type=F32)


def _bdot_tn(a, b):
    return lax.dot_general(a.astype(BF16), b.astype(BF16), (((0,), (0,)), ((), ())), preferred_element_type=F32)


def _split_bf16(x, parts):
    out = []
    for _ in range(parts):
        piece = x.astype(BF16)
        out.append(piece)
        x = x - piece.astype(F32)
    return out


def _head_sum(x):
    lo_lanes = _iota((1, LANES), 1) < HEAD_DIM
    lo_sum = jnp.sum(jnp.where(lo_lanes, x, 0.0), axis=-1, keepdims=True)
    hi_sum = jnp.sum(jnp.where(lo_lanes, 0.0, x), axis=-1, keepdims=True)
    return jnp.where(lo_lanes, lo_sum, hi_sum)


def _params(sem):
    return pltpu.CompilerParams(dimension_semantics=sem, vmem_limit_bytes=VMEM_LIMIT)


def _layer_spec(a, l, resident=False):
    zeros = (0,) * (a.ndim - 1)
    mode = dict(pipeline_mode=pl.Buffered(1)) if resident else {}
    return pl.BlockSpec((None,) + a.shape[1:], lambda *_: (l,) + zeros, **mode)


def _inproj_kernel(x_ref, g_ref, w_ref, cos_ref, sin_ref, rkv_ref, lo_ref, q_ref, k_ref, v_ref):
    hb = _rms(x_ref[...], g_ref[...]).astype(BF16)
    rkv_ref[...] = _dot(hb, w_ref[:, 0:RKV_W])
    lo_ref[...] = _dot(hb, w_ref[:, RKV_W:RKV_W + LORA_W])
    qkv = _dot(hb, w_ref[:, RKV_W + LORA_W:PROJ_W])
    cos = cos_ref[...]
    sin = sin_ref[...]
    first_half = (_iota((1, LANES), 1) % HEAD_DIM) < (HEAD_DIM // 2)

    def rope(t):
        rot = jnp.where(first_half, pltpu.roll(t, LANES - HEAD_DIM // 2, 1), pltpu.roll(t, HEAD_DIM // 2, 1))
        return t * cos + rot * sin

    for j in range(B_WIDTH // LANES):
        q_ref[:, j * LANES:(j + 1) * LANES] = rope(qkv[:, j * LANES:(j + 1) * LANES])
    k_ref[...] = rope(qkv[:, B_WIDTH:B_WIDTH + KV_WIDTH])
    v_ref[...] = qkv[:, B_WIDTH + KV_WIDTH:]


def _in_proj(x, gamma, w_all, l, cos, sin, tm):
    n = x.shape[0]
    nrope = cos.shape[0] // tm
    row = lambda w: pl.BlockSpec((tm, w), lambda i: (i, 0))
    rope_row = pl.BlockSpec((tm, LANES), lambda i: (i % nrope, 0))
    sds = lambda w: jax.ShapeDtypeStruct((n, w), F32)
    return pl.pallas_call(
        _inproj_kernel, grid=(n // tm,),
        in_specs=[row(D_MODEL), _layer_spec(gamma, l), _layer_spec(w_all, l, resident=True), rope_row, rope_row],
        out_specs=[row(RKV_W), row(LORA_W), row(B_WIDTH), row(KV_WIDTH), row(KV_WIDTH)],
        out_shape=[sds(RKV_W), sds(LORA_W), sds(B_WIDTH), sds(KV_WIDTH), sds(KV_WIDTH)],
        compiler_params=_params(("parallel",)), name="in_proj",
    )(x, gamma, w_all, cos, sin)


def _rwkv_prep(xs_rkv, xs_lo, w0, wd, a0, wa, wg, k_k, k_a):
    r = xs_rkv[:, 0:A_WIDTH]
    k = xs_rkv[:, A_WIDTH:2 * A_WIDTH]
    v = xs_rkv[:, 2 * A_WIDTH:3 * A_WIDTH]
    z = w0 + _bdot(jnp.tanh(xs_lo[:, 0:LORA_A_OFF]), wd)
    logw = -EXP_NEG_HALF * jax.nn.sigmoid(z)
    a = jax.nn.sigmoid(a0 + _bdot(xs_lo[:, LORA_A_OFF:LORA_G_OFF], wa))
    g = _bdot(jax.nn.sigmoid(xs_lo[:, LORA_G_OFF:LORA_W]), wg)
    kk = k * k_k
    parts = []
    for j in range(A_WIDTH // LANES):
        t = kk[:, j * LANES:(j + 1) * LANES]
        parts.append(t * lax.rsqrt(jnp.maximum(_head_sum(t * t), 1e-24)))
    kk = jnp.concatenate(parts, axis=1)
    k_mod = k * (1.0 + (a - 1.0) * k_a)
    return r, logw, k_mod, v, -kk, kk * a, g


def _rwkv_post(y, r, k_mod, v, g, r_k, lnx_g, lnx_b):
    parts = []
    for j in range(A_WIDTH // LANES):
        ln = slice(j * LANES, (j + 1) * LANES)
        yj = y[:, ln]
        mean = _head_sum(yj) * (1.0 / HEAD_DIM)
        d = yj - mean
        var = _head_sum(d * d) * (1.0 / HEAD_DIM)
        yn = d * lax.rsqrt(var + LNX_EPS) * lnx_g[:, ln] + lnx_b[:, ln]
        bonus = _head_sum(r[:, ln] * k_mod[:, ln] * r_k[:, ln]) * v[:, ln]
        parts.append((yn + bonus) * g[:, ln])
    return jnp.concatenate(parts, axis=1)


def _rwkv_chunk_kernel(rkv_ref, lo_ref, mu_rkv_ref, mu_lo_ref, w0_ref, wd_ref, a0_ref, wa_ref, wg_ref,
                       kk_ref, ka_ref, rk_ref, lng_ref, lnb_ref,
                       ya_ref, s_out_ref,
                       s_scr, carry_rkv, carry_lo, r_s, lw_s, km_s, v_s, an_s, bn_s, g_s, cum_s, y_s, *, nb, tb, nt):
    t = pl.program_id(0)

    @pl.when(t == 0)
    def _():
        for ref in (s_scr, carry_rkv, carry_lo, r_s, lw_s, km_s, v_s, an_s, bn_s, g_s, cum_s):
            ref[...] = jnp.zeros_like(ref)

    slot = t % 2
    done = 1 - slot

    row0 = _iota((tb, 1), 0) == 0
    ti = _iota((tb, tb), 0)
    tj = _iota((tb, tb), 1)
    tri = ((ti // CHUNK == tj // CHUNK) & (ti >= tj)).astype(BF16)
    for b in range(nb):
        rkv = rkv_ref[b]
        lo = lo_ref[b]
        tile = slice(SUBLANES * b, SUBLANES * (b + 1))
        last = slice(SUBLANES * (b + 1) - 1, SUBLANES * (b + 1))
        prev_rkv = jnp.where(row0, carry_rkv[last, :], pltpu.roll(rkv, 1, 0))
        prev_lo = jnp.where(row0, carry_lo[last, :], pltpu.roll(lo, 1, 0))
        carry_rkv[tile, :] = rkv[tb - SUBLANES:tb, :]
        carry_lo[tile, :] = lo[tb - SUBLANES:tb, :]
        xs_rkv = rkv + mu_rkv_ref[...] * (prev_rkv - rkv)
        xs_lo = lo + mu_lo_ref[...] * (prev_lo - lo)
        r, logw, k_mod, v, a_neg, b_pos, g = _rwkv_prep(
            xs_rkv, xs_lo, w0_ref[...], wd_ref[...], a0_ref[...], wa_ref[...], wg_ref[...], kk_ref[...], ka_ref[...])
        blk = slice(b * tb, (b + 1) * tb)
        r_s[slot, blk, :] = r
        lw_s[slot, blk, :] = logw
        km_s[slot, blk, :] = k_mod
        v_s[slot, blk, :] = v
        an_s[slot, blk, :] = a_neg
        bn_s[slot, blk, :] = b_pos
        g_s[slot, blk, :] = g
        cum_s[slot, blk, :] = sum(_dot(tri, piece) for piece in _split_bf16(logw, 3))

    ri = _iota((LANES, LANES), 0)
    rj = _iota((LANES, LANES), 1)
    same = (ri // CHUNK) == (rj // CHUNK)
    strict = same & (ri > rj)
    incl = same & (ri >= rj)
    incl2 = jnp.concatenate([incl, incl], axis=1)
    eye = (ri == rj).astype(F32)
    lo_lanes = _iota((1, LANES), 1) < HEAD_DIM

    def stack(x):
        x = x.astype(BF16)
        return jnp.concatenate([jnp.where(lo_lanes, x, 0.0), jnp.where(lo_lanes, 0.0, x)], axis=0)

    pairs = range(nb * HEAD_PAIRS)
    lanes = [slice((p % HEAD_PAIRS) * LANES, (p % HEAD_PAIRS + 1) * LANES) for p in pairs]
    s2 = [s_scr[p] for p in pairs]
    for c0 in range(0, tb // CHUNK, AHEAD_CHUNKS):
        items = [(c, p) for c in range(c0, c0 + AHEAD_CHUNKS) for p in pairs]
        rowss = {(c, p): slice((p // HEAD_PAIRS) * tb + c * CHUNK, (p // HEAD_PAIRS) * tb + (c + 1) * CHUNK)
                 for c, p in items}
        ar, bk_end, v2, w_end, gram = {}, {}, {}, {}, {}
        for it in items:
            ln = lanes[it[1]]
            rows = rowss[it]
            lw = lw_s[done, rows, ln]
            cum = cum_s[done, rows, ln]
            w_inv = jnp.exp(-cum)
            w_end[it] = jnp.exp(cum[CHUNK - 1:CHUNK, :])
            a2 = stack(an_s[done, rows, ln] * jnp.exp(cum - lw))
            r2 = stack(r_s[done, rows, ln] * jnp.exp(cum))
            b1 = bn_s[done, rows, ln] * w_inv
            k1 = km_s[done, rows, ln] * w_inv
            v2[it] = stack(v_s[done, rows, ln])
            ar[it] = jnp.concatenate([a2, r2], axis=0)
            bk = jnp.concatenate([stack(b1), stack(k1)], axis=0)
            bk_end[it] = jnp.concatenate([stack(b1 * w_end[it]), stack(k1 * w_end[it])], axis=0)
            gram[it] = _bdot_nt(ar[it], bk)
        a_ab = {it: jnp.where(strict, gram[it][0:LANES, 0:LANES], 0.0) for it in items}
        a_ak = {it: jnp.where(strict, gram[it][0:LANES, LANES:].astype(BF16), 0.0) for it in items}
        q_bk = {it: jnp.where(incl2, gram[it][LANES:, :].astype(BF16), 0.0) for it in items}
        npow = {it: _bdot(a_ab[it], a_ab[it]) for it in items}
        tinv = {it: eye + a_ab[it] for it in items}
        for _ in range(4):
            both = {it: _bdot(jnp.concatenate([npow[it], tinv[it]], axis=0), npow[it]) for it in items}
            npow = {it: both[it][0:LANES, :] for it in items}
            tinv = {it: tinv[it] + both[it][LANES:, :] for it in items}
        tinv = {it: tinv[it] + _bdot(tinv[it], npow[it]) for it in items}
        akv = {it: _bdot(a_ak[it], v2[it]) for it in items}
        for c in range(c0, c0 + AHEAD_CHUNKS):
            its = [(c, p) for p in pairs]
            ars = {it: _bdot_nt(ar[it], s2[it[1]]) for it in its}
            u2 = {it: _bdot(tinv[it], ars[it][0:LANES, :] + akv[it]) for it in its}
            uv = {it: jnp.concatenate([u2[it].astype(BF16), v2[it]], axis=0) for it in its}
            y2 = {it: ars[it][LANES:, :] + _bdot(q_bk[it], uv[it]) for it in its}
            for it in its:
                s2[it[1]] = s2[it[1]] * w_end[it] + _bdot_tn(uv[it], bk_end[it])
                y_s[rowss[it], lanes[it[1]]] = y2[it][0:CHUNK, :] + y2[it][CHUNK:, :]
    for p in pairs:
        s_scr[p] = s2[p]

    for b in range(nb):
        blk = slice(b * tb, (b + 1) * tb)
        ya_ref[b] = _rwkv_post(y_s[blk, :], r_s[done, blk, :], km_s[done, blk, :], v_s[done, blk, :],
                               g_s[done, blk, :], rk_ref[...], lng_ref[...], lnb_ref[...])

    @pl.when(t == nt)
    def _():
        for p in pairs:
            b, h = p // HEAD_PAIRS, 2 * (p % HEAD_PAIRS)
            s_out_ref[b, h] = s2[p][0:HEAD_DIM, 0:HEAD_DIM]
            s_out_ref[b, h + 1] = s2[p][HEAD_DIM:, HEAD_DIM:]


_RWKV_PARAMS = ("mu_rkv", "mu_lo", "w0", "wd", "a0", "wa", "wg", "k_k", "k_a", "r_k", "lnx_g", "lnx_b")


def _rwkv_chunk(rkv, lo, wts, l, nb, tb):
    n = rkv.shape[0]
    nt = n // nb // tb
    blk_in = lambda w: pl.BlockSpec((nb, tb, w), lambda t: (0, jnp.minimum(t, nt - 1), 0))
    blk_out = pl.BlockSpec((nb, tb, A_WIDTH), lambda t: (0, jnp.maximum(t - 1, 0), 0))
    consts = [wts[k] for k in _RWKV_PARAMS]
    wide = lambda: pltpu.VMEM((2, nb * tb, A_WIDTH), F32)
    ya, state = pl.pallas_call(
        functools.partial(_rwkv_chunk_kernel, nb=nb, tb=tb, nt=nt), grid=(nt + 1,),
        in_specs=[blk_in(RKV_W), blk_in(LORA_W)] + [_layer_spec(a, l) for a in consts],
        out_specs=[blk_out, pl.BlockSpec((nb, N_A_HEADS, HEAD_DIM, HEAD_DIM), lambda t: (0, 0, 0, 0))],
        out_shape=[jax.ShapeDtypeStruct((nb, n // nb, A_WIDTH), F32),
                   jax.ShapeDtypeStruct((nb, N_A_HEADS, HEAD_DIM, HEAD_DIM), F32)],
        scratch_shapes=[pltpu.VMEM((nb * HEAD_PAIRS, LANES, LANES), F32), pltpu.VMEM((SUBLANES * nb, RKV_W), F32),
                        pltpu.VMEM((SUBLANES * nb, LORA_W), F32)] + [wide() for _ in range(8)]
        + [pltpu.VMEM((nb * tb, A_WIDTH), F32)],
        compiler_params=_params(("arbitrary",)), name="rwkv_chunk",
    )(rkv.reshape(nb, n // nb, RKV_W), lo.reshape(nb, n // nb, LORA_W), *consts)
    return ya.reshape(n, A_WIDTH), state


I_UNROLL = 8


def _rwkv_step_kernel(rkv_ref, lo_ref, prkv_ref, plo_ref, s_ref, mu_rkv_ref, mu_lo_ref, w0_ref, wd_ref, a0_ref,
                      wa_ref, wg_ref, kk_ref, ka_ref, rk_ref, lng_ref, lnb_ref, *rest):
    ya_ref, s_out_ref, r_t, w_t, km_t, v_t, an_t, bn_t, y_t, r_s, km_s, v_s, g_s = rest[-13:]
    h = pl.program_id(0)

    @pl.when(h == 0)
    def _():
        rkv = rkv_ref[...]
        lo = lo_ref[...]
        xs_rkv = rkv + mu_rkv_ref[...] * (prkv_ref[...] - rkv)
        xs_lo = lo + mu_lo_ref[...] * (plo_ref[...] - lo)
        r, logw, k_mod, v, a_neg, b_pos, g = _rwkv_prep(
            xs_rkv, xs_lo, w0_ref[...], wd_ref[...], a0_ref[...], wa_ref[...], wg_ref[...], kk_ref[...],
            ka_ref[...])
        for ref, val in ((r_t, r), (w_t, jnp.exp(logw)), (km_t, k_mod), (v_t, v), (an_t, a_neg), (bn_t, b_pos)):
            ref[...] = val.T
        r_s[...] = r
        km_s[...] = k_mod
        v_s[...] = v
        g_s[...] = g

    head = pl.ds(pl.multiple_of(h * HEAD_DIM, HEAD_DIM), HEAD_DIM)
    r_h, w_h, k_h, a_h, b_h = r_t[head, :], w_t[head, :], km_t[head, :], an_t[head, :], bn_t[head, :]

    def rows(g_i, carry):
        ids = [g_i * I_UNROLL + u for u in range(I_UNROLL)]
        vrow = [v_t[pl.ds(h * HEAD_DIM + i, 1), :] for i in ids]
        s0 = [s_ref[i] for i in ids]
        sa = [jnp.sum(s0[u] * a_h, axis=0, keepdims=True) for u in range(I_UNROLL)]
        s1 = [s0[u] * w_h + sa[u] * b_h + vrow[u] * k_h for u in range(I_UNROLL)]
        for u, i in enumerate(ids):
            s_out_ref[i] = s1[u]
            y_t[pl.ds(h * HEAD_DIM + i, 1), :] = jnp.sum(s1[u] * r_h, axis=0, keepdims=True)
        return carry

    lax.fori_loop(0, HEAD_DIM // I_UNROLL, rows, 0)

    @pl.when(h == N_A_HEADS - 1)
    def _():
        ya_ref[...] = _rwkv_post(y_t[...].T, r_s[...], km_s[...], v_s[...], g_s[...],
                                 rk_ref[...], lng_ref[...], lnb_ref[...])


def _rwkv_step(rkv, lo, prev_rkv, prev_lo, state_t, new_state_t, wts, l):
    n = rkv.shape[0]
    full = lambda a: pl.BlockSpec(a.shape, lambda h: (0,) * a.ndim)
    prev = lambda w: pl.BlockSpec((None, n, w), lambda h: (l, 0, 0))
    st = pl.BlockSpec((None, None, HEAD_DIM, HEAD_DIM, n), lambda h: (l, h, 0, 0, 0))
    consts = [wts[k] for k in _RWKV_PARAMS]
    return pl.pallas_call(
        _rwkv_step_kernel, grid=(N_A_HEADS,),
        in_specs=[full(rkv), full(lo), prev(RKV_W), prev(LORA_W), st] + [_layer_spec(a, l) for a in consts]
        + ([] if new_state_t is None else [pl.BlockSpec(memory_space=pl.ANY)]),
        out_specs=[pl.BlockSpec((n, A_WIDTH), lambda h: (0, 0)), st],
        out_shape=[jax.ShapeDtypeStruct((n, A_WIDTH), F32), jax.ShapeDtypeStruct(state_t.shape, F32)],
        input_output_aliases={} if new_state_t is None else {5 + len(consts): 1},
        scratch_shapes=[pltpu.VMEM((A_WIDTH, n), F32) for _ in range(7)]
        + [pltpu.VMEM((n, A_WIDTH), F32) for _ in range(4)],
        compiler_params=_params(("arbitrary",)), name="rwkv_step",
    )(rkv, lo, prev_rkv, prev_lo, state_t, *consts, *([] if new_state_t is None else [new_state_t]))


def _dup_kv(x):
    lo_lanes = _iota((1, LANES), 1) < HEAD_DIM
    xr = pltpu.roll(x, HEAD_DIM, 1)
    return [jnp.where(lo_lanes, x, xr), jnp.where(lo_lanes, xr, x)]


def _sink_attention(q, kdup, vdup, sink_ref, l, mask):
    rows = q.shape[0]
    lo_lanes = _iota((1, LANES), 1) < HEAD_DIM
    head_of_row = _iota((GQA_GROUP * rows, 1), 0) // rows
    mask4 = jnp.concatenate([mask] * GQA_GROUP, axis=0)
    groups = range(N_KV_HEADS)
    qs, sink = [], []
    for g in groups:
        parts = []
        sink_g = jnp.zeros((GQA_GROUP * rows, 1), F32)
        for r in range(GQA_GROUP):
            h = g * GQA_GROUP + r
            q128 = q[:, (h // 2) * LANES:(h // 2 + 1) * LANES] * ATTN_SCALE
            parts.append(jnp.where(lo_lanes, q128, 0.0) if h % 2 == 0 else jnp.where(lo_lanes, 0.0, q128))
            sink_g = jnp.where(head_of_row == r, sink_ref[l, h], sink_g)
        qs.append(jnp.concatenate(parts, axis=0))
        sink.append(sink_g)
    s = [jnp.where(mask4, _bdot_nt(qs[g], kdup[g]), -jnp.inf) for g in groups]
    m = [jnp.maximum(jnp.max(s[g], axis=-1, keepdims=True), sink[g]) for g in groups]
    p = [jnp.exp(s[g] - m[g]) for g in groups]
    den = [jnp.sum(p[g], axis=-1, keepdims=True) + jnp.exp(sink[g] - m[g]) for g in groups]
    o = [_bdot(p[g], vdup[g]) * (1.0 / den[g]) for g in groups]
    outs = []
    for g in groups:
        for jj in range(GQA_GROUP // 2):
            even = o[g][(2 * jj) * rows:(2 * jj + 1) * rows, :]
            odd = o[g][(2 * jj + 1) * rows:(2 * jj + 2) * rows, :]
            outs.append(jnp.where(lo_lanes, even, odd))
    return jnp.concatenate(outs, axis=1)


def _swa_banded_kernel(sink_ref, q_ref, kc_ref, kp_ref, vc_ref, vp_ref, o_ref, *, l):
    n = pl.program_id(1)
    kall = jnp.concatenate([kp_ref[...], kc_ref[...]], axis=0)
    vall = jnp.concatenate([vp_ref[...], vc_ref[...]], axis=0)
    qi = _iota((WINDOW, 2 * WINDOW), 0)
    kj = _iota((WINDOW, 2 * WINDOW), 1)
    diff = qi - kj + WINDOW
    band = (diff >= 0) & (diff < WINDOW)
    masks = [band & ((kj >= WINDOW) | (n > 0)), band]
    for j in range(2):
        keys = slice(j * WINDOW, (j + 2) * WINDOW)
        rows = slice(j * WINDOW, (j + 1) * WINDOW)
        o_ref[rows, :] = _sink_attention(q_ref[rows, :], _dup_kv(kall[keys, :]), _dup_kv(vall[keys, :]),
                                         sink_ref, l, masks[j])


def _swa_banded(q, k, v, sinks, l, nb):
    n = q.shape[0]
    nstep = n // nb // (2 * WINDOW)
    cur = lambda w: pl.BlockSpec((2 * WINDOW, w), lambda b, i: (b * nstep + i, 0))
    prev = lambda w: pl.BlockSpec((WINDOW, w), lambda b, i: (2 * (b * nstep + i) - jnp.minimum(i, 1), 0))
    return pl.pallas_call(
        functools.partial(_swa_banded_kernel, l=l), grid=(nb, nstep),
        in_specs=[pl.BlockSpec(memory_space=pltpu.SMEM), cur(B_WIDTH), cur(KV_WIDTH), prev(KV_WIDTH),
                  cur(KV_WIDTH), prev(KV_WIDTH)],
        out_specs=cur(B_WIDTH),
        out_shape=jax.ShapeDtypeStruct((n, B_WIDTH), F32),
        compiler_params=_params(("parallel", "parallel")), name="swa_banded",
    )(sinks, q, k, k, v, v)


def _swa_step_kernel(sink_ref, q_ref, k_ref, v_ref, kb_ref, vb_ref, o_ref, ko_ref, vo_ref, *, l, bb):
    last = _iota((WINDOW, 1), 0) == WINDOW - 1
    head = _iota((N_Q_HEADS, 1), 0)
    lane = _iota((1, LANES), 1)
    own_half = (lane < HEAD_DIM) == (head % 2 == 0)
    own_keys = (_iota((1, 2 * WINDOW), 1) // WINDOW) == (head // GQA_GROUP)
    sink = jnp.zeros((N_Q_HEADS, 1), F32)
    for h in range(N_Q_HEADS):
        sink = jnp.where(head == h, sink_ref[l, h], sink)
    lo_lanes = lane < HEAD_DIM

    rows = range(bb)
    kcat, vcat, q8 = [], [], []
    for bi in rows:
        kc = jnp.where(last, k_ref[bi:bi + 1, :], pltpu.roll(kb_ref[bi], WINDOW - 1, 0))
        vc = jnp.where(last, v_ref[bi:bi + 1, :], pltpu.roll(vb_ref[bi], WINDOW - 1, 0))
        ko_ref[bi] = kc
        vo_ref[bi] = vc
        kcat.append(jnp.concatenate(_dup_kv(kc), axis=0))
        vcat.append(jnp.concatenate(_dup_kv(vc), axis=0))
        q = q_ref[bi:bi + 1, :]
        qsel = q[:, 0:LANES]
        for j in range(1, B_WIDTH // LANES):
            qsel = jnp.where(head // 2 == j, q[:, j * LANES:(j + 1) * LANES], qsel)
        q8.append(jnp.where(own_half, qsel, 0.0))
    s = [jnp.where(own_keys, _dot_nt(q8[bi], kcat[bi]) * ATTN_SCALE, -jnp.inf) for bi in rows]
    m = [jnp.maximum(jnp.max(s[bi], axis=-1, keepdims=True), sink) for bi in rows]
    p = [jnp.exp(s[bi] - m[bi]) for bi in rows]
    den = [jnp.sum(p[bi], axis=-1, keepdims=True) + jnp.exp(sink - m[bi]) for bi in rows]
    o = [_dot(p[bi], vcat[bi]) / den[bi] for bi in rows]
    for bi in rows:
        for j in range(B_WIDTH // LANES):
            o_ref[bi:bi + 1, j * LANES:(j + 1) * LANES] = jnp.where(
                lo_lanes, o[bi][2 * j:2 * j + 1, :], o[bi][2 * j + 1:2 * j + 2, :])


def _swa_step(q, k, v, kbuf, vbuf, sinks, l, bb):
    n = q.shape[0]
    row = lambda w: pl.BlockSpec((bb, w), lambda i: (i, 0))
    buf = pl.BlockSpec((None, bb, WINDOW, KV_WIDTH), lambda i: (l, i, 0, 0))
    return pl.pallas_call(
        functools.partial(_swa_step_kernel, l=l, bb=bb), grid=(n // bb,),
        in_specs=[pl.BlockSpec(memory_space=pltpu.SMEM), row(B_WIDTH), row(KV_WIDTH), row(KV_WIDTH), buf, buf],
        out_specs=[row(B_WIDTH), buf, buf],
        out_shape=[jax.ShapeDtypeStruct((n, B_WIDTH), F32), jax.ShapeDtypeStruct(kbuf.shape, F32),
                   jax.ShapeDtypeStruct(vbuf.shape, F32)],
        input_output_aliases={4: 1, 5: 2},
        compiler_params=_params(("parallel",)), name="swa_step",
    )(sinks, q, k, v, kbuf, vbuf)


def _mix_ffn_kernel(x_ref, ya_ref, yb_ref, wo_ref, g_ref, wg_ref, wu_ref, wd_ref, fg_ref, o_ref, *, tf, final):
    x = (x_ref[...] + _dot(ya_ref[...].astype(BF16), wo_ref[0:A_WIDTH, :])
         + _dot(yb_ref[...].astype(BF16), wo_ref[A_WIDTH:, :]))
    h = _rms(x, g_ref[...]).astype(BF16)
    o_ref[...] = x
    for f in range(D_FF // tf):
        cols = slice(f * tf, (f + 1) * tf)
        gate = _dot(h, wg_ref[:, cols])
        up = _dot(h, wu_ref[:, cols])
        act = (gate * jax.nn.sigmoid(gate) * up).astype(BF16)
        o_ref[...] += _dot(act, wd_ref[cols, :])
    if final:
        o_ref[...] = _rms(o_ref[...], fg_ref[...])


def _mix_ffn(x, ya, yb, wts, l, final_gamma, final, tm, tf):
    n = x.shape[0]
    row = lambda w: pl.BlockSpec((tm, w), lambda i: (i, 0))
    weights = [wts[k] for k in ("w_out", "ffn_norm", "w_gate", "w_up", "w_down")]
    return pl.pallas_call(
        functools.partial(_mix_ffn_kernel, tf=tf, final=final), grid=(n // tm,),
        in_specs=[row(D_MODEL), row(A_WIDTH), row(B_WIDTH)] + [_layer_spec(a, l, resident=True) for a in weights]
        + [pl.BlockSpec(final_gamma.shape, lambda i: (0, 0))],
        out_specs=row(D_MODEL), out_shape=jax.ShapeDtypeStruct((n, D_MODEL), F32),
        compiler_params=_params(("parallel",)), name="mix_ffn",
    )(x, ya, yb, *weights, final_gamma)


def _pad_lora_cols(t):
    z = lambda n: jnp.zeros(t.shape[:-1] + (n,), t.dtype)
    d, a = DECAY_LORA, DECAY_LORA + AAA_LORA
    return jnp.concatenate([t[..., 0:d], z(LORA_A_OFF - DECAY_LORA), t[..., d:a], z(LORA_G_OFF - LORA_A_OFF - AAA_LORA),
                            t[..., a:a + GATE_LORA], z(LORA_W - LORA_G_OFF - GATE_LORA)], axis=-1)


def _unpad_lora_cols(t):
    return jnp.concatenate([t[..., 0:DECAY_LORA], t[..., LORA_A_OFF:LORA_A_OFF + AAA_LORA],
                            t[..., LORA_G_OFF:LORA_G_OFF + GATE_LORA]], axis=-1)


def _pad_rows(t, rows):
    return jnp.concatenate([t, jnp.zeros(t.shape[:-2] + (rows - t.shape[-2], t.shape[-1]), t.dtype)], axis=-2)


def _rope_tables(positions):
    half = HEAD_DIM // 2
    inv = ROPE_THETA ** (-jnp.arange(half, dtype=F32) / half)
    ang = positions.astype(F32)[:, None] * inv[None, :]
    cos = jnp.cos(ang)
    sin = jnp.sin(ang)
    return jnp.tile(cos, (1, LANES // half)), jnp.tile(jnp.concatenate([-sin, sin], axis=1), (1, LANES // HEAD_DIM))


def _prep_weights(attn_norm, w_in, mu, w0, w_decay_up, a0, w_a_up, w_g_up, k_k, k_a, r_k, lnx_g, lnx_b, sinks,
                  w_out, ffn_norm, w_gate, w_up, w_down):
    depth = w_in.shape[0]
    vec = lambda t: t.reshape(depth, 1, -1)
    w_in = w_in.astype(BF16)
    w_all = jnp.concatenate([w_in[..., 0:RKV_W], _pad_lora_cols(w_in[..., RKV_W:A_PROJ]), w_in[..., A_PROJ:]], axis=-1)
    return dict(
        attn_norm=vec(attn_norm), w_all=w_all,
        mu_rkv=vec(mu[:, 0:RKV_W]), mu_lo=vec(_pad_lora_cols(mu[:, RKV_W:])),
        w0=vec(w0), wd=_pad_rows(w_decay_up, LORA_A_OFF), a0=vec(a0),
        wa=_pad_rows(w_a_up, LORA_G_OFF - LORA_A_OFF), wg=_pad_rows(w_g_up, LORA_W - LORA_G_OFF),
        k_k=vec(k_k), k_a=vec(k_a), r_k=vec(r_k), lnx_g=vec(lnx_g), lnx_b=vec(lnx_b),
        sinks=sinks, w_out=w_out.astype(BF16), ffn_norm=vec(ffn_norm),
        w_gate=w_gate.astype(BF16), w_up=w_up.astype(BF16), w_down=w_down.astype(BF16))


def _last_pa_row(rkv, lo, nb):
    rkv_last = rkv.reshape(nb, -1, RKV_W)[:, -1]
    lo_last = lo.reshape(nb, -1, LORA_W)[:, -1]
    return jnp.concatenate([rkv_last, _unpad_lora_cols(lo_last)], axis=-1)


def kernel(x_prompt, x_sample, state_rwkv, state_shift, cache_k_win, cache_v_win, attn_norm, w_in, mu, w0, w_decay_up, a0, w_a_up, w_g_up, k_k, k_a, r_k, lnx_g, lnx_b, sinks, w_out, ffn_norm, w_gate, w_up, w_down, final_norm):
    bp, tp, _ = x_prompt.shape
    bs, ts, _ = x_sample.shape
    depth = w_in.shape[0]
    assert ts == 1, "sample kernels handle exactly one new token per sequence"
    tm_p = 512 if tp % 512 == 0 else tp
    tm_in = 1024 if tp % 1024 == 0 else tm_p
    tb = 256
    tf = 256
    sample_bb = 8

    cos_p, sin_p = _rope_tables(jnp.arange(tp, dtype=jnp.int32))
    cos_s, sin_s = _rope_tables(jnp.full((bs,), PAST_LEN, dtype=jnp.int32))
    final_g = final_norm.reshape(1, -1)

    wts = _prep_weights(attn_norm, w_in, mu, w0, w_decay_up, a0, w_a_up, w_g_up, k_k, k_a, r_k, lnx_g, lnx_b, sinks,
                        w_out, ffn_norm, w_gate, w_up, w_down)
    prev_rkv = state_shift[..., 0:RKV_W]
    prev_lo = _pad_lora_cols(state_shift[..., RKV_W:])
    state_t = jnp.transpose(state_rwkv, (0, 2, 3, 4, 1))
    s_S = None
    s_k = cache_k_win.reshape(depth, bs, WINDOW, KV_WIDTH)
    s_v = cache_v_win.reshape(depth, bs, WINDOW, KV_WIDTH)

    xp = x_prompt.reshape(bp * tp, D_MODEL)
    xs = x_sample.reshape(bs * ts, D_MODEL)
    p_S, p_sh, p_k, p_v, s_sh = [], [], [], [], []
    for l in range(depth):
        final = l == depth - 1

        rkv, lo, q, k, v = _in_proj(xp, wts["attn_norm"], wts["w_all"], l, cos_p, sin_p, tm_in)
        ya, S = _rwkv_chunk(rkv, lo, wts, l, bp, tb)
        yb = _swa_banded(q, k, v, wts["sinks"], l, bp)
        xp = _mix_ffn(xp, ya, yb, wts, l, final_g, final, tm_p, tf)
        p_S.append(S)
        p_sh.append(_last_pa_row(rkv, lo, bp))
        p_k.append(k.reshape(bp, tp, KV_WIDTH)[:, tp - WINDOW:].reshape(bp, WINDOW, N_KV_HEADS, HEAD_DIM))
        p_v.append(v.reshape(bp, tp, KV_WIDTH)[:, tp - WINDOW:].reshape(bp, WINDOW, N_KV_HEADS, HEAD_DIM))

        rkv, lo, q, k, v = _in_proj(xs, wts["attn_norm"], wts["w_all"], l, cos_s, sin_s, bs)
        ya, s_S = _rwkv_step(rkv, lo, prev_rkv, prev_lo, state_t, s_S, wts, l)
        yb, s_k, s_v = _swa_step(q, k, v, s_k, s_v, wts["sinks"], l, sample_bb)
        xs = _mix_ffn(xs, ya, yb, wts, l, final_g, final, bs, tf)
        s_sh.append(_last_pa_row(rkv, lo, bs))

    cache_shape = (depth, bs, WINDOW, N_KV_HEADS, HEAD_DIM)
    return (xp.reshape(bp, tp, D_MODEL), xs.reshape(bs, ts, D_MODEL),
            jnp.stack(p_S), jnp.stack(p_sh), jnp.stack(p_k), jnp.stack(p_v),
            jnp.transpose(s_S, (0, 4, 1, 2, 3)), jnp.stack(s_sh), s_k.reshape(cache_shape), s_v.reshape(cache_shape))
```

```python
import functools

import jax
import jax.numpy as jnp
from jax import lax
from jax.experimental import pallas as pl
from jax.experimental.pallas import tpu as pltpu

F32 = jnp.float32
BF16 = jnp.bfloat16

D_MODEL = 1024
HEAD_DIM = 64
N_A_HEADS = 8
A_WIDTH = N_A_HEADS * HEAD_DIM
N_Q_HEADS = 8
N_KV_HEADS = 2
GQA_GROUP = N_Q_HEADS // N_KV_HEADS
B_WIDTH = N_Q_HEADS * HEAD_DIM
KV_WIDTH = N_KV_HEADS * HEAD_DIM
DECAY_LORA = 64
AAA_LORA = 64
GATE_LORA = 160
A_PROJ = 3 * A_WIDTH + DECAY_LORA + AAA_LORA + GATE_LORA
WINDOW = 128
PAST_LEN = 8192
D_FF = 2816
ROPE_THETA = 10000.0
NORM_EPS = 1e-5
LNX_EPS = 64e-5
ATTN_SCALE = HEAD_DIM ** -0.5
EXP_NEG_HALF = 0.6065306597126334

LANES = 128
SUBLANES = 8
HEAD_PAIRS = N_A_HEADS // 2
CHUNK = 64
AHEAD_CHUNKS = 2
RKV_W = 3 * A_WIDTH
LORA_W = 512
LORA_A_OFF = 128
LORA_G_OFF = 256
PROJ_W = RKV_W + LORA_W + B_WIDTH + 2 * KV_WIDTH
VMEM_LIMIT = 56 * 1024 * 1024


def _dot(a, b):
    return jnp.dot(a, b, preferred_element_type=F32)


def _dot_nt(a, b):
    return lax.dot_general(a, b, (((1,), (1,)), ((), ())), preferred_element_type=F32)


def _iota(shape, dim):
    return lax.broadcasted_iota(jnp.int32, shape, dim)


def _rms(x, g):
    return x * lax.rsqrt(jnp.mean(x * x, axis=-1, keepdims=True) + NORM_EPS) * g


def _bdot(a, b):
    return jnp.dot(a.astype(BF16), b.astype(BF16), preferred_element_type=F32)


def _bdot_nt(a, b):
    return lax.dot_general(a.astype(BF16), b.astype(BF16), (((1,), (1,)), ((), ())), preferred_element_type=F32)


def _bdot_tn(a, b):
    return lax.dot_general(a.astype(BF16), b.astype(BF16), (((0,), (0,)), ((), ())), preferred_element_type=F32)


def _split_bf16(x, parts):
    out = []
    for _ in range(parts):
        piece = x.astype(BF16)
        out.append(piece)
        x = x - piece.astype(F32)
    return out


def _head_sum(x):
    lo_lanes = _iota((1, LANES), 1) < HEAD_DIM
    lo_sum = jnp.sum(jnp.where(lo_lanes, x, 0.0), axis=-1, keepdims=True)
    hi_sum = jnp.sum(jnp.where(lo_lanes, 0.0, x), axis=-1, keepdims=True)
    return jnp.where(lo_lanes, lo_sum, hi_sum)


def _params(sem):
    return pltpu.CompilerParams(dimension_semantics=sem, vmem_limit_bytes=VMEM_LIMIT)


def _layer_spec(a, l, resident=False):
    zeros = (0,) * (a.ndim - 1)
    mode = dict(pipeline_mode=pl.Buffered(1)) if resident else {}
    return pl.BlockSpec((None,) + a.shape[1:], lambda *_: (l,) + zeros, **mode)


def _inproj_kernel(x_ref, g_ref, w_ref, cos_ref, sin_ref, rkv_ref, lo_ref, q_ref, k_ref, v_ref):
    hb = _rms(x_ref[...], g_ref[...]).astype(BF16)
    rkv_ref[...] = _dot(hb, w_ref[:, 0:RKV_W])
    lo_ref[...] = _dot(hb, w_ref[:, RKV_W:RKV_W + LORA_W])
    qkv = _dot(hb, w_ref[:, RKV_W + LORA_W:PROJ_W])
    cos = cos_ref[...]
    sin = sin_ref[...]
    first_half = (_iota((1, LANES), 1) % HEAD_DIM) < (HEAD_DIM // 2)

    def rope(t):
        rot = jnp.where(first_half, pltpu.roll(t, LANES - HEAD_DIM // 2, 1), pltpu.roll(t, HEAD_DIM // 2, 1))
        return t * cos + rot * sin

    for j in range(B_WIDTH // LANES):
        q_ref[:, j * LANES:(j + 1) * LANES] = rope(qkv[:, j * LANES:(j + 1) * LANES])
    k_ref[...] = rope(qkv[:, B_WIDTH:B_WIDTH + KV_WIDTH])
    v_ref[...] = qkv[:, B_WIDTH + KV_WIDTH:]


def _in_proj(x, gamma, w_all, l, cos, sin, tm):
    n = x.shape[0]
    nrope = cos.shape[0] // tm
    row = lambda w: pl.BlockSpec((tm, w), lambda i: (i, 0))
    rope_row = pl.BlockSpec((tm, LANES), lambda i: (i % nrope, 0))
    sds = lambda w: jax.ShapeDtypeStruct((n, w), F32)
    return pl.pallas_call(
        _inproj_kernel, grid=(n // tm,),
        in_specs=[row(D_MODEL), _layer_spec(gamma, l), _layer_spec(w_all, l, resident=True), rope_row, rope_row],
        out_specs=[row(RKV_W), row(LORA_W), row(B_WIDTH), row(KV_WIDTH), row(KV_WIDTH)],
        out_shape=[sds(RKV_W), sds(LORA_W), sds(B_WIDTH), sds(KV_WIDTH), sds(KV_WIDTH)],
        compiler_params=_params(("parallel",)), name="in_proj",
    )(x, gamma, w_all, cos, sin)


def _rwkv_prep(xs_rkv, xs_lo, w0, wd, a0, wa, wg, k_k, k_a):
    r = xs_rkv[:, 0:A_WIDTH]
    k = xs_rkv[:, A_WIDTH:2 * A_WIDTH]
    v = xs_rkv[:, 2 * A_WIDTH:3 * A_WIDTH]
    z = w0 + _bdot(jnp.tanh(xs_lo[:, 0:LORA_A_OFF]), wd)
    logw = -EXP_NEG_HALF * jax.nn.sigmoid(z)
    a = jax.nn.sigmoid(a0 + _bdot(xs_lo[:, LORA_A_OFF:LORA_G_OFF], wa))
    g = _bdot(jax.nn.sigmoid(xs_lo[:, LORA_G_OFF:LORA_W]), wg)
    kk = k * k_k
    parts = []
    for j in range(A_WIDTH // LANES):
        t = kk[:, j * LANES:(j + 1) * LANES]
        parts.append(t * lax.rsqrt(jnp.maximum(_head_sum(t * t), 1e-24)))
    kk = jnp.concatenate(parts, axis=1)
    k_mod = k * (1.0 + (a - 1.0) * k_a)
    return r, logw, k_mod, v, -kk, kk * a, g


def _rwkv_post(y, r, k_mod, v, g, r_k, lnx_g, lnx_b):
    parts = []
    for j in range(A_WIDTH // LANES):
        ln = slice(j * LANES, (j + 1) * LANES)
        yj = y[:, ln]
        mean = _head_sum(yj) * (1.0 / HEAD_DIM)
        d = yj - mean
        var = _head_sum(d * d) * (1.0 / HEAD_DIM)
        yn = d * lax.rsqrt(var + LNX_EPS) * lnx_g[:, ln] + lnx_b[:, ln]
        bonus = _head_sum(r[:, ln] * k_mod[:, ln] * r_k[:, ln]) * v[:, ln]
        parts.append((yn + bonus) * g[:, ln])
    return jnp.concatenate(parts, axis=1)


def _rwkv_chunk_kernel(rkv_ref, lo_ref, mu_rkv_ref, mu_lo_ref, w0_ref, wd_ref, a0_ref, wa_ref, wg_ref,
                       kk_ref, ka_ref, rk_ref, lng_ref, lnb_ref,
                       ya_ref, s_out_ref,
                       s_scr, carry_rkv, carry_lo, r_s, lw_s, km_s, v_s, an_s, bn_s, g_s, cum_s, y_s, *, nb, tb, nt):
    t = pl.program_id(0)

    @pl.when(t == 0)
    def _():
        for ref in (s_scr, carry_rkv, carry_lo, r_s, lw_s, km_s, v_s, an_s, bn_s, g_s, cum_s):
            ref[...] = jnp.zeros_like(ref)

    slot = t % 2
    done = 1 - slot

    row0 = _iota((tb, 1), 0) == 0
    ti = _iota((tb, tb), 0)
    tj = _iota((tb, tb), 1)
    tri = ((ti // CHUNK == tj // CHUNK) & (ti >= tj)).astype(BF16)
    for b in range(nb):
        rkv = rkv_ref[b]
        lo = lo_ref[b]
        tile = slice(SUBLANES * b, SUBLANES * (b + 1))
        last = slice(SUBLANES * (b + 1) - 1, SUBLANES * (b + 1))
        prev_rkv = jnp.where(row0, carry_rkv[last, :], pltpu.roll(rkv, 1, 0))
        prev_lo = jnp.where(row0, carry_lo[last, :], pltpu.roll(lo, 1, 0))
        carry_rkv[tile, :] = rkv[tb - SUBLANES:tb, :]
        carry_lo[tile, :] = lo[tb - SUBLANES:tb, :]
        xs_rkv = rkv + mu_rkv_ref[...] * (prev_rkv - rkv)
        xs_lo = lo + mu_lo_ref[...] * (prev_lo - lo)
        r, logw, k_mod, v, a_neg, b_pos, g = _rwkv_prep(
            xs_rkv, xs_lo, w0_ref[...], wd_ref[...], a0_ref[...], wa_ref[...], wg_ref[...], kk_ref[...], ka_ref[...])
        blk = slice(b * tb, (b + 1) * tb)
        r_s[slot, blk, :] = r
        lw_s[slot, blk, :] = logw
        km_s[slot, blk, :] = k_mod
        v_s[slot, blk, :] = v
        an_s[slot, blk, :] = a_neg
        bn_s[slot, blk, :] = b_pos
        g_s[slot, blk, :] = g
        cum_s[slot, blk, :] = sum(_dot(tri, piece) for piece in _split_bf16(logw, 3))

    ri = _iota((LANES, LANES), 0)
    rj = _iota((LANES, LANES), 1)
    same = (ri // CHUNK) == (rj // CHUNK)
    strict = same & (ri > rj)
    incl = same & (ri >= rj)
    incl2 = jnp.concatenate([incl, incl], axis=1)
    eye = (ri == rj).astype(F32)
    lo_lanes = _iota((1, LANES), 1) < HEAD_DIM

    def stack(x):
        x = x.astype(BF16)
        return jnp.concatenate([jnp.where(lo_lanes, x, 0.0), jnp.where(lo_lanes, 0.0, x)], axis=0)

    pairs = range(nb * HEAD_PAIRS)
    lanes = [slice((p % HEAD_PAIRS) * LANES, (p % HEAD_PAIRS + 1) * LANES) for p in pairs]
    s2 = [s_scr[p] for p in pairs]
    for c0 in range(0, tb // CHUNK, AHEAD_CHUNKS):
        items = [(c, p) for c in range(c0, c0 + AHEAD_CHUNKS) for p in pairs]
        rowss = {(c, p): slice((p // HEAD_PAIRS) * tb + c * CHUNK, (p // HEAD_PAIRS) * tb + (c + 1) * CHUNK)
                 for c, p in items}
        ar, bk_end, v2, w_end, gram = {}, {}, {}, {}, {}
        for it in items:
            ln = lanes[it[1]]
            rows = rowss[it]
            lw = lw_s[done, rows, ln]
            cum = cum_s[done, rows, ln]
            w_inv = jnp.exp(-cum)
            w_end[it] = jnp.exp(cum[CHUNK - 1:CHUNK, :])
            a2 = stack(an_s[done, rows, ln] * jnp.exp(cum - lw))
            r2 = stack(r_s[done, rows, ln] * jnp.exp(cum))
            b1 = bn_s[done, rows, ln] * w_inv
            k1 = km_s[done, rows, ln] * w_inv
            v2[it] = stack(v_s[done, rows, ln])
            ar[it] = jnp.concatenate([a2, r2], axis=0)
            bk = jnp.concatenate([stack(b1), stack(k1)], axis=0)
            bk_end[it] = jnp.concatenate([stack(b1 * w_end[it]), stack(k1 * w_end[it])], axis=0)
            gram[it] = _bdot_nt(ar[it], bk)
        a_ab = {it: jnp.where(strict, gram[it][0:LANES, 0:LANES], 0.0) for it in items}
        a_ak = {it: jnp.where(strict, gram[it][0:LANES, LANES:].astype(BF16), 0.0) for it in items}
        q_bk = {it: jnp.where(incl2, gram[it][LANES:, :].astype(BF16), 0.0) for it in items}
        npow = {it: _bdot(a_ab[it], a_ab[it]) for it in items}
        tinv = {it: eye + a_ab[it] for it in items}
        for _ in range(4):
            both = {it: _bdot(jnp.concatenate([npow[it], tinv[it]], axis=0), npow[it]) for it in items}
            npow = {it: both[it][0:LANES, :] for it in items}
            tinv = {it: tinv[it] + both[it][LANES:, :] for it in items}
        tinv = {it: tinv[it] + _bdot(tinv[it], npow[it]) for it in items}
        akv = {it: _bdot(a_ak[it], v2[it]) for it in items}
        for c in range(c0, c0 + AHEAD_CHUNKS):
            its = [(c, p) for p in pairs]
            ars = {it: _bdot_nt(ar[it], s2[it[1]]) for it in its}
            u2 = {it: _bdot(tinv[it], ars[it][0:LANES, :] + akv[it]) for it in its}
            uv = {it: jnp.concatenate([u2[it].astype(BF16), v2[it]], axis=0) for it in its}
            y2 = {it: ars[it][LANES:, :] + _bdot(q_bk[it], uv[it]) for it in its}
            for it in its:
                s2[it[1]] = s2[it[1]] * w_end[it] + _bdot_tn(uv[it], bk_end[it])
                y_s[rowss[it], lanes[it[1]]] = y2[it][0:CHUNK, :] + y2[it][CHUNK:, :]
    for p in pairs:
        s_scr[p] = s2[p]

    for b in range(nb):
        blk = slice(b * tb, (b + 1) * tb)
        ya_ref[b] = _rwkv_post(y_s[blk, :], r_s[done, blk, :], km_s[done, blk, :], v_s[done, blk, :],
                               g_s[done, blk, :], rk_ref[...], lng_ref[...], lnb_ref[...])

    @pl.when(t == nt)
    def _():
        for p in pairs:
            b, h = p // HEAD_PAIRS, 2 * (p % HEAD_PAIRS)
            s_out_ref[b, h] = s2[p][0:HEAD_DIM, 0:HEAD_DIM]
            s_out_ref[b, h + 1] = s2[p][HEAD_DIM:, HEAD_DIM:]


_RWKV_PARAMS = ("mu_rkv", "mu_lo", "w0", "wd", "a0", "wa", "wg", "k_k", "k_a", "r_k", "lnx_g", "lnx_b")


def _rwkv_chunk(rkv, lo, wts, l, nb, tb):
    n = rkv.shape[0]
    nt = n // nb // tb
    blk_in = lambda w: pl.BlockSpec((nb, tb, w), lambda t: (0, jnp.minimum(t, nt - 1), 0))
    blk_out = pl.BlockSpec((nb, tb, A_WIDTH), lambda t: (0, jnp.maximum(t - 1, 0), 0))
    consts = [wts[k] for k in _RWKV_PARAMS]
    wide = lambda: pltpu.VMEM((2, nb * tb, A_WIDTH), F32)
    ya, state = pl.pallas_call(
        functools.partial(_rwkv_chunk_kernel, nb=nb, tb=tb, nt=nt), grid=(nt + 1,),
        in_specs=[blk_in(RKV_W), blk_in(LORA_W)] + [_layer_spec(a, l) for a in consts],
        out_specs=[blk_out, pl.BlockSpec((nb, N_A_HEADS, HEAD_DIM, HEAD_DIM), lambda t: (0, 0, 0, 0))],
        out_shape=[jax.ShapeDtypeStruct((nb, n // nb, A_WIDTH), F32),
                   jax.ShapeDtypeStruct((nb, N_A_HEADS, HEAD_DIM, HEAD_DIM), F32)],
        scratch_shapes=[pltpu.VMEM((nb * HEAD_PAIRS, LANES, LANES), F32), pltpu.VMEM((SUBLANES * nb, RKV_W), F32),
                        pltpu.VMEM((SUBLANES * nb, LORA_W), F32)] + [wide() for _ in range(8)]
        + [pltpu.VMEM((nb * tb, A_WIDTH), F32)],
        compiler_params=_params(("arbitrary",)), name="rwkv_chunk",
    )(rkv.reshape(nb, n // nb, RKV_W), lo.reshape(nb, n // nb, LORA_W), *consts)
    return ya.reshape(n, A_WIDTH), state


I_UNROLL = 8


def _rwkv_step_kernel(rkv_ref, lo_ref, prkv_ref, plo_ref, s_ref, mu_rkv_ref, mu_lo_ref, w0_ref, wd_ref, a0_ref,
                      wa_ref, wg_ref, kk_ref, ka_ref, rk_ref, lng_ref, lnb_ref, *rest):
    ya_ref, s_out_ref, r_t, w_t, km_t, v_t, an_t, bn_t, y_t, r_s, km_s, v_s, g_s = rest[-13:]
    h = pl.program_id(0)

    @pl.when(h == 0)
    def _():
        rkv = rkv_ref[...]
        lo = lo_ref[...]
        xs_rkv = rkv + mu_rkv_ref[...] * (prkv_ref[...] - rkv)
        xs_lo = lo + mu_lo_ref[...] * (plo_ref[...] - lo)
        r, logw, k_mod, v, a_neg, b_pos, g = _rwkv_prep(
            xs_rkv, xs_lo, w0_ref[...], wd_ref[...], a0_ref[...], wa_ref[...], wg_ref[...], kk_ref[...],
            ka_ref[...])
        for ref, val in ((r_t, r), (w_t, jnp.exp(logw)), (km_t, k_mod), (v_t, v), (an_t, a_neg), (bn_t, b_pos)):
            ref[...] = val.T
        r_s[...] = r
        km_s[...] = k_mod
        v_s[...] = v
        g_s[...] = g

    head = pl.ds(pl.multiple_of(h * HEAD_DIM, HEAD_DIM), HEAD_DIM)
    r_h, w_h, k_h, a_h, b_h = r_t[head, :], w_t[head, :], km_t[head, :], an_t[head, :], bn_t[head, :]

    def rows(g_i, carry):
        ids = [g_i * I_UNROLL + u for u in range(I_UNROLL)]
        vrow = [v_t[pl.ds(h * HEAD_DIM + i, 1), :] for i in ids]
        s0 = [s_ref[i] for i in ids]
        sa = [jnp.sum(s0[u] * a_h, axis=0, keepdims=True) for u in range(I_UNROLL)]
        s1 = [s0[u] * w_h + sa[u] * b_h + vrow[u] * k_h for u in range(I_UNROLL)]
        for u, i in enumerate(ids):
            s_out_ref[i] = s1[u]
            y_t[pl.ds(h * HEAD_DIM + i, 1), :] = jnp.sum(s1[u] * r_h, axis=0, keepdims=True)
        return carry

    lax.fori_loop(0, HEAD_DIM // I_UNROLL, rows, 0)

    @pl.when(h == N_A_HEADS - 1)
    def _():
        ya_ref[...] = _rwkv_post(y_t[...].T, r_s[...], km_s[...], v_s[...], g_s[...],
                                 rk_ref[...], lng_ref[...], lnb_ref[...])


def _rwkv_step(rkv, lo, prev_rkv, prev_lo, state_t, new_state_t, wts, l):
    n = rkv.shape[0]
    full = lambda a: pl.BlockSpec(a.shape, lambda h: (0,) * a.ndim)
    prev = lambda w: pl.BlockSpec((None, n, w), lambda h: (l, 0, 0))
    st = pl.BlockSpec((None, None, HEAD_DIM, HEAD_DIM, n), lambda h: (l, h, 0, 0, 0))
    consts = [wts[k] for k in _RWKV_PARAMS]
    return pl.pallas_call(
        _rwkv_step_kernel, grid=(N_A_HEADS,),
        in_specs=[full(rkv), full(lo), prev(RKV_W), prev(LORA_W), st] + [_layer_spec(a, l) for a in consts]
        + ([] if new_state_t is None else [pl.BlockSpec(memory_space=pl.ANY)]),
        out_specs=[pl.BlockSpec((n, A_WIDTH), lambda h: (0, 0)), st],
        out_shape=[jax.ShapeDtypeStruct((n, A_WIDTH), F32), jax.ShapeDtypeStruct(state_t.shape, F32)],
        input_output_aliases={} if new_state_t is None else {5 + len(consts): 1},
        scratch_shapes=[pltpu.VMEM((A_WIDTH, n), F32) for _ in range(7)]
        + [pltpu.VMEM((n, A_WIDTH), F32) for _ in range(4)],
        compiler_params=_params(("arbitrary",)), name="rwkv_step",
    )(rkv, lo, prev_rkv, prev_lo, state_t, *consts, *([] if new_state_t is None else [new_state_t]))


def _dup_kv(x):
    lo_lanes = _iota((1, LANES), 1) < HEAD_DIM
    xr = pltpu.roll(x, HEAD_DIM, 1)
    return [jnp.where(lo_lanes, x, xr), jnp.where(lo_lanes, xr, x)]


def _sink_attention(q, kdup, vdup, sink_ref, l, mask):
    rows = q.shape[0]
    lo_lanes = _iota((1, LANES), 1) < HEAD_DIM
    head_of_row = _iota((GQA_GROUP * rows, 1), 0) // rows
    mask4 = jnp.concatenate([mask] * GQA_GROUP, axis=0)
    groups = range(N_KV_HEADS)
    qs, sink = [], []
    for g in groups:
        parts = []
        sink_g = jnp.zeros((GQA_GROUP * rows, 1), F32)
        for r in range(GQA_GROUP):
            h = g * GQA_GROUP + r
            q128 = q[:, (h // 2) * LANES:(h // 2 + 1) * LANES] * ATTN_SCALE
            parts.append(jnp.where(lo_lanes, q128, 0.0) if h % 2 == 0 else jnp.where(lo_lanes, 0.0, q128))
            sink_g = jnp.where(head_of_row == r, sink_ref[l, h], sink_g)
        qs.append(jnp.concatenate(parts, axis=0))
        sink.append(sink_g)
    s = [jnp.where(mask4, _bdot_nt(qs[g], kdup[g]), -jnp.inf) for g in groups]
    m = [jnp.maximum(jnp.max(s[g], axis=-1, keepdims=True), sink[g]) for g in groups]
    p = [jnp.exp(s[g] - m[g]) for g in groups]
    den = [jnp.sum(p[g], axis=-1, keepdims=True) + jnp.exp(sink[g] - m[g]) for g in groups]
    o = [_bdot(p[g], vdup[g]) * (1.0 / den[g]) for g in groups]
    outs = []
    for g in groups:
        for jj in range(GQA_GROUP // 2):
            even = o[g][(2 * jj) * rows:(2 * jj + 1) * rows, :]
            odd = o[g][(2 * jj + 1) * rows:(2 * jj + 2) * rows, :]
            outs.append(jnp.where(lo_lanes, even, odd))
    return jnp.concatenate(outs, axis=1)


def _swa_banded_kernel(sink_ref, q_ref, kc_ref, kp_ref, vc_ref, vp_ref, o_ref, *, l):
    n = pl.program_id(1)
    kall = jnp.concatenate([kp_ref[...], kc_ref[...]], axis=0)
    vall = jnp.concatenate([vp_ref[...], vc_ref[...]], axis=0)
    qi = _iota((WINDOW, 2 * WINDOW), 0)
    kj = _iota((WINDOW, 2 * WINDOW), 1)
    diff = qi - kj + WINDOW
    band = (diff >= 0) & (diff < WINDOW)
    masks = [band & ((kj >= WINDOW) | (n > 0)), band]
    for j in range(2):
        keys = slice(j * WINDOW, (j + 2) * WINDOW)
        rows = slice(j * WINDOW, (j + 1) * WINDOW)
        o_ref[rows, :] = _sink_attention(q_ref[rows, :], _dup_kv(kall[keys, :]), _dup_kv(vall[keys, :]),
                                         sink_ref, l, masks[j])


def _swa_banded(q, k, v, sinks, l, nb):
    n = q.shape[0]
    nstep = n // nb // (2 * WINDOW)
    cur = lambda w: pl.BlockSpec((2 * WINDOW, w), lambda b, i: (b * nstep + i, 0))
    prev = lambda w: pl.BlockSpec((WINDOW, w), lambda b, i: (2 * (b * nstep + i) - jnp.minimum(i, 1), 0))
    return pl.pallas_call(
        functools.partial(_swa_banded_kernel, l=l), grid=(nb, nstep),
        in_specs=[pl.BlockSpec(memory_space=pltpu.SMEM), cur(B_WIDTH), cur(KV_WIDTH), prev(KV_WIDTH),
                  cur(KV_WIDTH), prev(KV_WIDTH)],
        out_specs=cur(B_WIDTH),
        out_shape=jax.ShapeDtypeStruct((n, B_WIDTH), F32),
        compiler_params=_params(("parallel", "parallel")), name="swa_banded",
    )(sinks, q, k, k, v, v)


def _swa_step_kernel(sink_ref, q_ref, k_ref, v_ref, kb_ref, vb_ref, o_ref, ko_ref, vo_ref, *, l, bb):
    last = _iota((WINDOW, 1), 0) == WINDOW - 1
    head = _iota((N_Q_HEADS, 1), 0)
    lane = _iota((1, LANES), 1)
    own_half = (lane < HEAD_DIM) == (head % 2 == 0)
    own_keys = (_iota((1, 2 * WINDOW), 1) // WINDOW) == (head // GQA_GROUP)
    sink = jnp.zeros((N_Q_HEADS, 1), F32)
    for h in range(N_Q_HEADS):
        sink = jnp.where(head == h, sink_ref[l, h], sink)
    lo_lanes = lane < HEAD_DIM

    rows = range(bb)
    kcat, vcat, q8 = [], [], []
    for bi in rows:
        kc = jnp.where(last, k_ref[bi:bi + 1, :], pltpu.roll(kb_ref[bi], WINDOW - 1, 0))
        vc = jnp.where(last, v_ref[bi:bi + 1, :], pltpu.roll(vb_ref[bi], WINDOW - 1, 0))
        ko_ref[bi] = kc
        vo_ref[bi] = vc
        kcat.append(jnp.concatenate(_dup_kv(kc), axis=0))
        vcat.append(jnp.concatenate(_dup_kv(vc), axis=0))
        q = q_ref[bi:bi + 1, :]
        qsel = q[:, 0:LANES]
        for j in range(1, B_WIDTH // LANES):
            qsel = jnp.where(head // 2 == j, q[:, j * LANES:(j + 1) * LANES], qsel)
        q8.append(jnp.where(own_half, qsel, 0.0))
    s = [jnp.where(own_keys, _dot_nt(q8[bi], kcat[bi]) * ATTN_SCALE, -jnp.inf) for bi in rows]
    m = [jnp.maximum(jnp.max(s[bi], axis=-1, keepdims=True), sink) for bi in rows]
    p = [jnp.exp(s[bi] - m[bi]) for bi in rows]
    den = [jnp.sum(p[bi], axis=-1, keepdims=True) + jnp.exp(sink - m[bi]) for bi in rows]
    o = [_dot(p[bi], vcat[bi]) / den[bi] for bi in rows]
    for bi in rows:
        for j in range(B_WIDTH // LANES):
            o_ref[bi:bi + 1, j * LANES:(j + 1) * LANES] = jnp.where(
                lo_lanes, o[bi][2 * j:2 * j + 1, :], o[bi][2 * j + 1:2 * j + 2, :])


def _swa_step(q, k, v, kbuf, vbuf, sinks, l, bb):
    n = q.shape[0]
    row = lambda w: pl.BlockSpec((bb, w), lambda i: (i, 0))
    buf = pl.BlockSpec((None, bb, WINDOW, KV_WIDTH), lambda i: (l, i, 0, 0))
    return pl.pallas_call(
        functools.partial(_swa_step_kernel, l=l, bb=bb), grid=(n // bb,),
        in_specs=[pl.BlockSpec(memory_space=pltpu.SMEM), row(B_WIDTH), row(KV_WIDTH), row(KV_WIDTH), buf, buf],
        out_specs=[row(B_WIDTH), buf, buf],
        out_shape=[jax.ShapeDtypeStruct((n, B_WIDTH), F32), jax.ShapeDtypeStruct(kbuf.shape, F32),
                   jax.ShapeDtypeStruct(vbuf.shape, F32)],
        input_output_aliases={4: 1, 5: 2},
        compiler_params=_params(("parallel",)), name="swa_step",
    )(sinks, q, k, v, kbuf, vbuf)


def _mix_ffn_kernel(x_ref, ya_ref, yb_ref, wo_ref, g_ref, wg_ref, wu_ref, wd_ref, fg_ref, o_ref, *, tf, final):
    x = (x_ref[...] + _dot(ya_ref[...].astype(BF16), wo_ref[0:A_WIDTH, :])
         + _dot(yb_ref[...].astype(BF16), wo_ref[A_WIDTH:, :]))
    h = _rms(x, g_ref[...]).astype(BF16)
    o_ref[...] = x
    for f in range(D_FF // tf):
        cols = slice(f * tf, (f + 1) * tf)
        gate = _dot(h, wg_ref[:, cols])
        up = _dot(h, wu_ref[:, cols])
        act = (gate * jax.nn.sigmoid(gate) * up).astype(BF16)
        o_ref[...] += _dot(act, wd_ref[cols, :])
    if final:
        o_ref[...] = _rms(o_ref[...], fg_ref[...])


def _mix_ffn(x, ya, yb, wts, l, final_gamma, final, tm, tf):
    n = x.shape[0]
    row = lambda w: pl.BlockSpec((tm, w), lambda i: (i, 0))
    weights = [wts[k] for k in ("w_out", "ffn_norm", "w_gate", "w_up", "w_down")]
    return pl.pallas_call(
        functools.partial(_mix_ffn_kernel, tf=tf, final=final), grid=(n // tm,),
        in_specs=[row(D_MODEL), row(A_WIDTH), row(B_WIDTH)] + [_layer_spec(a, l, resident=True) for a in weights]
        + [pl.BlockSpec(final_gamma.shape, lambda i: (0, 0))],
        out_specs=row(D_MODEL), out_shape=jax.ShapeDtypeStruct((n, D_MODEL), F32),
        compiler_params=_params(("parallel",)), name="mix_ffn",
    )(x, ya, yb, *weights, final_gamma)


def _pad_lora_cols(t):
    z = lambda n: jnp.zeros(t.shape[:-1] + (n,), t.dtype)
    d, a = DECAY_LORA, DECAY_LORA + AAA_LORA
    return jnp.concatenate([t[..., 0:d], z(LORA_A_OFF - DECAY_LORA), t[..., d:a], z(LORA_G_OFF - LORA_A_OFF - AAA_LORA),
                            t[..., a:a + GATE_LORA], z(LORA_W - LORA_G_OFF - GATE_LORA)], axis=-1)


def _unpad_lora_cols(t):
    return jnp.concatenate([t[..., 0:DECAY_LORA], t[..., LORA_A_OFF:LORA_A_OFF + AAA_LORA],
                            t[..., LORA_G_OFF:LORA_G_OFF + GATE_LORA]], axis=-1)


def _pad_rows(t, rows):
    return jnp.concatenate([t, jnp.zeros(t.shape[:-2] + (rows - t.shape[-2], t.shape[-1]), t.dtype)], axis=-2)


def _rope_tables(positions):
    half = HEAD_DIM // 2
    inv = ROPE_THETA ** (-jnp.arange(half, dtype=F32) / half)
    ang = positions.astype(F32)[:, None] * inv[None, :]
    cos = jnp.cos(ang)
    sin = jnp.sin(ang)
    return jnp.tile(cos, (1, LANES // half)), jnp.tile(jnp.concatenate([-sin, sin], axis=1), (1, LANES // HEAD_DIM))


def _prep_weights(attn_norm, w_in, mu, w0, w_decay_up, a0, w_a_up, w_g_up, k_k, k_a, r_k, lnx_g, lnx_b, sinks,
                  w_out, ffn_norm, w_gate, w_up, w_down):
    depth = w_in.shape[0]
    vec = lambda t: t.reshape(depth, 1, -1)
    w_in = w_in.astype(BF16)
    w_all = jnp.concatenate([w_in[..., 0:RKV_W], _pad_lora_cols(w_in[..., RKV_W:A_PROJ]), w_in[..., A_PROJ:]], axis=-1)
    return dict(
        attn_norm=vec(attn_norm), w_all=w_all,
        mu_rkv=vec(mu[:, 0:RKV_W]), mu_lo=vec(_pad_lora_cols(mu[:, RKV_W:])),
        w0=vec(w0), wd=_pad_rows(w_decay_up, LORA_A_OFF), a0=vec(a0),
        wa=_pad_rows(w_a_up, LORA_G_OFF - LORA_A_OFF), wg=_pad_rows(w_g_up, LORA_W - LORA_G_OFF),
        k_k=vec(k_k), k_a=vec(k_a), r_k=vec(r_k), lnx_g=vec(lnx_g), lnx_b=vec(lnx_b),
        sinks=sinks, w_out=w_out.astype(BF16), ffn_norm=vec(ffn_norm),
        w_gate=w_gate.astype(BF16), w_up=w_up.astype(BF16), w_down=w_down.astype(BF16))


def _last_pa_row(rkv, lo, nb):
    rkv_last = rkv.reshape(nb, -1, RKV_W)[:, -1]
    lo_last = lo.reshape(nb, -1, LORA_W)[:, -1]
    return jnp.concatenate([rkv_last, _unpad_lora_cols(lo_last)], axis=-1)


def kernel(x_prompt, x_sample, state_rwkv, state_shift, cache_k_win, cache_v_win, attn_norm, w_in, mu, w0, w_decay_up, a0, w_a_up, w_g_up, k_k, k_a, r_k, lnx_g, lnx_b, sinks, w_out, ffn_norm, w_gate, w_up, w_down, final_norm):
    bp, tp, _ = x_prompt.shape
    bs, ts, _ = x_sample.shape
    depth = w_in.shape[0]
    assert ts == 1, "sample kernels handle exactly one new token per sequence"
    tm_p = 512 if tp % 512 == 0 else tp
    tm_in = 1024 if tp % 1024 == 0 else tm_p
    tb = 256
    tf = 256
    sample_bb = 8

    cos_p, sin_p = _rope_tables(jnp.arange(tp, dtype=jnp.int32))
    cos_s, sin_s = _rope_tables(jnp.full((bs,), PAST_LEN, dtype=jnp.int32))
    final_g = final_norm.reshape(1, -1)

    wts = _prep_weights(attn_norm, w_in, mu, w0, w_decay_up, a0, w_a_up, w_g_up, k_k, k_a, r_k, lnx_g, lnx_b, sinks,
                        w_out, ffn_norm, w_gate, w_up, w_down)
    prev_rkv = state_shift[..., 0:RKV_W]
    prev_lo = _pad_lora_cols(state_shift[..., RKV_W:])
    state_t = jnp.transpose(state_rwkv, (0, 2, 3, 4, 1))
    s_S = None
    s_k = cache_k_win.reshape(depth, bs, WINDOW, KV_WIDTH)
    s_v = cache_v_win.reshape(depth, bs, WINDOW, KV_WIDTH)

    xp = x_prompt.reshape(bp * tp, D_MODEL)
    xs = x_sample.reshape(bs * ts, D_MODEL)
    p_S, p_sh, p_k, p_v, s_sh = [], [], [], [], []
    for l in range(depth):
        final = l == depth - 1

        rkv, lo, q, k, v = _in_proj(xp, wts["attn_norm"], wts["w_all"], l, cos_p, sin_p, tm_in)
        ya, S = _rwkv_chunk(rkv, lo, wts, l, bp, tb)
        yb = _swa_banded(q, k, v, wts["sinks"], l, bp)
        xp = _mix_ffn(xp, ya, yb, wts, l, final_g, final, tm_in, tf)
        p_S.append(S)
        p_sh.append(_last_pa_row(rkv, lo, bp))
        p_k.append(k.reshape(bp, tp, KV_WIDTH)[:, tp - WINDOW:].reshape(bp, WINDOW, N_KV_HEADS, HEAD_DIM))
        p_v.append(v.reshape(bp, tp, KV_WIDTH)[:, tp - WINDOW:].reshape(bp, WINDOW, N_KV_HEADS, HEAD_DIM))

        rkv, lo, q, k, v = _in_proj(xs, wts["attn_norm"], wts["w_all"], l, cos_s, sin_s, bs)
        ya, s_S = _rwkv_step(rkv, lo, prev_rkv, prev_lo, state_t, s_S, wts, l)
        yb, s_k, s_v = _swa_step(q, k, v, s_k, s_v, wts["sinks"], l, sample_bb)
        xs = _mix_ffn(xs, ya, yb, wts, l, final_g, final, bs, tf)
        s_sh.append(_last_pa_row(rkv, lo, bs))

    cache_shape = (depth, bs, WINDOW, N_KV_HEADS, HEAD_DIM)
    return (xp.reshape(bp, tp, D_MODEL), xs.reshape(bs, ts, D_MODEL),
            jnp.stack(p_S), jnp.stack(p_sh), jnp.stack(p_k), jnp.stack(p_v),
            jnp.transpose(s_S, (0, 4, 1, 2, 3)), jnp.stack(s_sh), s_k.reshape(cache_shape), s_v.reshape(cache_shape))
```
